```python
import functools
import math
import jax, jax.numpy as jnp
from jax import lax
import numpy as np

D_MODEL = 2048
BATCH = 2
SEQ = 4096
DEPTH = 4
DEC_BATCH = 8
DEC_SEQ = 8
PAST_LEN = 16384
PAGE_SIZE = 128

D_CONV = D_MODEL // 2
CONV_W = 3
D_POOL = D_MODEL // 2
POOL_WINDOWS = (2, 4, 8, 16)
N_POOL_GROUPS = len(POOL_WINDOWS)
POOL_GROUP = D_POOL // N_POOL_GROUPS
POOL_STATE = max(POOL_WINDOWS) - 1
HEAD_DIM = 128
N_HEADS = D_MODEL // HEAD_DIM
N_KV = N_HEADS // 4
Q_PER_KV = N_HEADS // N_KV
D_ATTN = N_HEADS * HEAD_DIM
CMP_STRIDE = 16
CMP_BLOCK = 2 * CMP_STRIDE
CMP_HIDDEN = HEAD_DIM
SEL_BLOCK = 64
N_SEL = 16
WINDOW = 512
Q_BLOCK = 64
N_BUCKETS = 32
MAX_DISTANCE = 128
D_MIX = D_CONV + D_POOL + D_ATTN
D_FF = 4 * D_MODEL
RMS_EPS = 1e-6
NEG_BIG = -1e9
POS_BIG = 1e9
IN_SIZES = (D_CONV, D_CONV, D_CONV, D_POOL, D_ATTN, 6 * N_KV * HEAD_DIM, 3 * N_HEADS, 3 * D_MODEL)
N_IN = sum(IN_SIZES)

kernel_name = 'hybrid_conv_pool_nsa_decoder_step'


def rmsnorm(x, g):
    xf = x.astype(jnp.float32)
    y = xf * lax.rsqrt(jnp.mean(xf * xf, axis=-1, keepdims=True) + RMS_EPS)
    return y.astype(x.dtype) * g


def t5_bucket(dist):
    n = jnp.maximum(dist, 0)
    exact = N_BUCKETS // 2
    nf = jnp.maximum(n, 1).astype(jnp.float32)
    large = exact + (jnp.log(nf / exact) / math.log(MAX_DISTANCE / exact) * (N_BUCKETS - exact)).astype(jnp.int32)
    return jnp.where(n < exact, n, jnp.minimum(large, N_BUCKETS - 1))


def head_bias(rel_bias, qpos, kpos):
    b = rel_bias[t5_bucket(qpos[:, None] - kpos[None, :])]
    tq, nk = b.shape[0], b.shape[1]
    return jnp.transpose(b.reshape(tq, nk, N_KV, Q_PER_KV), (2, 3, 0, 1)).astype(jnp.float32)


def masked_softmax(s, mask):
    s = jnp.where(mask, s.astype(jnp.float32), -jnp.inf)
    m = jnp.max(s, axis=-1, keepdims=True)
    m = jnp.where(jnp.isfinite(m), m, 0.0)
    e = jnp.where(mask, jnp.exp(s - m), 0.0)
    return e / jnp.maximum(jnp.sum(e, axis=-1, keepdims=True), 1e-30)


def compress_kv(rows, pe, w1, w2):
    b, L = rows.shape[0], rows.shape[1]
    n16 = L // CMP_STRIDE
    ch = rows[:, :n16 * CMP_STRIDE].reshape(b, n16, CMP_STRIDE, 2, N_KV, HEAD_DIM)
    w1r = w1.reshape(2, CMP_BLOCK, HEAD_DIM, CMP_HIDDEN)
    hid = (jnp.einsum('bcpygd,ypdh->bcygh', ch[:, :-1], w1r[:, :CMP_STRIDE])
           + jnp.einsum('bcpygd,ypdh->bcygh', ch[:, 1:], w1r[:, CMP_STRIDE:])
           + jnp.einsum('ypd,ypdh->yh', pe, w1r)[:, None, :])
    out = jnp.einsum('bcygh,yhd->bcygd', jax.nn.gelu(hid), w2)
    return out[:, :, 0], out[:, :, 1]


def nsa_core(q, qpos, gates, kc, vc, fetch, n_slc, kw, kwpos, rel_bias):
    B, Tq = q.shape[0], q.shape[1]
    scale = HEAD_DIM ** -0.5
    nc = kc.shape[1]
    cend = jnp.arange(nc, dtype=jnp.int32) * CMP_STRIDE + (CMP_BLOCK - 1)
    s_c = jnp.einsum('bqgrd,bcgd->bgrqc', q, kc) * scale + head_bias(rel_bias, qpos, cend)
    p_c = masked_softmax(s_c, cend[None, :] <= qpos[:, None])
    o_c = jnp.einsum('bgrqc,bcgd->bqgrd', p_c.astype(q.dtype), vc)
    ratio = SEL_BLOCK // CMP_STRIDE
    lead = CMP_BLOCK // CMP_STRIDE - 1
    imp = jnp.sum(p_c, axis=2)
    imp = jnp.pad(imp, ((0, 0), (0, 0), (0, 0), (lead, ratio * (n_slc + 1) - nc)))
    p_slc = imp[..., 0::ratio][..., :n_slc]
    for o in range(1, ratio + lead):
        p_slc = p_slc + imp[..., o::ratio][..., :n_slc]
    cur = qpos // SEL_BLOCK
    j = jnp.arange(n_slc, dtype=jnp.int32)[None, :]
    forced = (j == 0) | (j == cur[:, None]) | (j == cur[:, None] - 1)
    score = jnp.where(j > cur[:, None], NEG_BIG, jnp.where(forced, POS_BIG, p_slc))
    vals, idx = lax.top_k(score, min(N_SEL, n_slc))
    blk_ok = vals > NEG_BIG / 2
    ks, vs, kpos = fetch(idx)
    tok_ok = jnp.repeat(blk_ok, SEL_BLOCK, axis=-1) & (kpos <= qpos[:, None])
    tab = jnp.transpose(rel_bias.reshape(N_BUCKETS, N_KV, Q_PER_KV), (1, 0, 2))
    gi = jnp.arange(N_KV)[None, :, None, None]
    bias_s = jnp.moveaxis(tab[gi, t5_bucket(qpos[:, None] - kpos)], -1, 2).astype(jnp.float32)
    s_s = jnp.einsum('bqgrd,bgqkd->bgrqk', q, ks) * scale + bias_s
    p_s = masked_softmax(s_s, tok_ok[:, :, None])
    o_s = jnp.einsum('bgrqk,bgqkd->bqgrd', p_s.astype(q.dtype), vs)
    s_w = jnp.einsum('bqgrd,bkgd->bgrqk', q, kw[:, :, 0]) * scale + head_bias(rel_bias, qpos, kwpos)
    dist = qpos[:, None] - kwpos[None, :]
    p_w = masked_softmax(s_w, (dist >= 0) & (dist < WINDOW) & (kwpos[None, :] >= 0))
    o_w = jnp.einsum('bgrqk,bkgd->bqgrd', p_w.astype(q.dtype), kw[:, :, 1])
    out = gates[..., 0:1] * o_c + gates[..., 1:2] * o_s + gates[..., 2:3] * o_w
    return out.reshape(B, Tq, D_ATTN)


def nsa_prompt(pe, w1, w2, rel_bias, q, kv, gates):
    B, T = q.shape[0], q.shape[1]
    kc, vc = compress_kv(kv[:, :, 0], pe, w1, w2)
    slc = kv[:, :, 1]
    n_slc = -(-T // SEL_BLOCK)
    bi = jnp.arange(B)[:, None, None, None, None, None]
    gi = jnp.arange(N_KV)[None, :, None, None, None, None]
    yi = jnp.arange(2)[None, None, None, None, None, :]

    def fetch(idx):
        tok = idx[..., None] * SEL_BLOCK + jnp.arange(SEL_BLOCK, dtype=jnp.int32)
        rows = slc[bi, jnp.clip(tok, 0, T - 1)[..., None], yi, gi]
        sh = tok.shape
        rows = rows.reshape(sh[0], sh[1], sh[2], sh[3] * sh[4], 2, HEAD_DIM)
        return rows[..., 0, :], rows[..., 1, :], tok.reshape(sh[0], sh[1], sh[2], sh[3] * sh[4])

    kw_pad = jnp.pad(kv[:, :, 2], ((0, 0), (WINDOW, 0), (0, 0), (0, 0), (0, 0)))

    def block(i):
        s0 = i * Q_BLOCK
        qb = lax.dynamic_slice_in_dim(q, s0, Q_BLOCK, axis=1)
        gb = lax.dynamic_slice_in_dim(gates, s0, Q_BLOCK, axis=1)
        kw = lax.dynamic_slice_in_dim(kw_pad, s0, Q_BLOCK + WINDOW, axis=1)
        qpos = s0 + jnp.arange(Q_BLOCK, dtype=jnp.int32)
        kwpos = s0 - WINDOW + jnp.arange(Q_BLOCK + WINDOW, dtype=jnp.int32)
        return nsa_core(qb, qpos, gb, kc, vc, fetch, n_slc, kw, kwpos, rel_bias)

    out = lax.map(block, jnp.arange(T // Q_BLOCK, dtype=jnp.int32))
    out = jnp.moveaxis(out, 0, 1).reshape(B, T, D_ATTN)
    wb = min(WINDOW, T)
    return out, (kv[:, :, 0], slc, kv[:, T - wb:, 2])


def nsa_sample(pe, w1, w2, rel_bias, cmp_pool, slc_pool, page_table, win_buf, q, kv, gates):
    B, S = q.shape[0], q.shape[1]
    n_pages = page_table.shape[1]
    past = n_pages * PAGE_SIZE
    cmp_past = cmp_pool[page_table].reshape(B, past, 2, N_KV, HEAD_DIM)
    kc, vc = compress_kv(jnp.concatenate([cmp_past, kv[:, :, 0]], axis=1), pe, w1, w2)
    n_slc = -(-(past + S) // SEL_BLOCK)
    new = kv[:, :, 1]
    bi = jnp.arange(B)[:, None, None, None, None, None]
    gi = jnp.arange(N_KV)[None, :, None, None, None, None]
    yi = jnp.arange(2)[None, None, None, None, None, :]

    def fetch(idx):
        tok = idx[..., None] * SEL_BLOCK + jnp.arange(SEL_BLOCK, dtype=jnp.int32)
        t6 = tok[..., None]
        page = jnp.clip(t6 // PAGE_SIZE, 0, n_pages - 1)
        phys = page_table[bi, page]
        from_past = slc_pool[phys, t6 % PAGE_SIZE, yi, gi]
        from_new = new[bi, jnp.clip(t6 - past, 0, S - 1), yi, gi]
        rows = jnp.where((t6 < past)[..., None], from_past, from_new)
        sh = tok.shape
        rows = rows.reshape(sh[0], sh[1], sh[2], sh[3] * sh[4], 2, HEAD_DIM)
        return rows[..., 0, :], rows[..., 1, :], tok.reshape(sh[0], sh[1], sh[2], sh[3] * sh[4])

    wb = win_buf.shape[1]
    kw = jnp.concatenate([win_buf.astype(kv.dtype), kv[:, :, 2]], axis=1)
    kwpos = past - wb + jnp.arange(wb + S, dtype=jnp.int32)
    qpos = past + jnp.arange(S, dtype=jnp.int32)
    out = nsa_core(q, qpos, gates, kc, vc, fetch, n_slc, kw, kwpos, rel_bias)
    return out, (kv[:, :, 0], new, kw[:, -wb:])


def short_conv(u, prefix, w):
    T = u.shape[1]
    ue = jnp.concatenate([prefix.astype(u.dtype), u], axis=1)
    v = w[0] * ue[:, 0:T]
    for k in range(1, CONV_W):
        v = v + w[k] * ue[:, k:k + T]
    return v, ue[:, -(CONV_W - 1):]


def pool_mixer(p, prefix, pos, pool_w, pool_scale):
    B, T = p.shape[0], p.shape[1]
    P = prefix.shape[1]
    pe = jnp.concatenate([prefix.astype(p.dtype), p], axis=1)
    cs = jnp.cumsum(pe.astype(jnp.float32), axis=1)
    cs0 = jnp.concatenate([jnp.zeros((B, 1, D_POOL), jnp.float32), cs], axis=1)
    outs = []
    for g, w in enumerate(POOL_WINDOWS):
        lo, hi = g * POOL_GROUP, (g + 1) * POOL_GROUP
        win_sum = cs0[:, P + 1:P + 1 + T, lo:hi] - cs0[:, P + 1 - w:P + 1 - w + T, lo:hi]
        cnt = jnp.minimum(pos + 1, w).astype(jnp.float32)[None, :, None]
        outs.append(win_sum / cnt - p[:, :, lo:hi].astype(jnp.float32))
    d = jnp.concatenate(outs, axis=-1).astype(p.dtype).reshape(B, T, N_POOL_GROUPS, POOL_GROUP)
    y = jnp.einsum('btgc,gce->btge', d, pool_w).reshape(B, T, D_POOL) * pool_scale
    return y, pe[:, -POOL_STATE:]


def mixer_sublayer(h, lp, conv_prefix, pool_prefix, pos, nsa_fn):
    B, T = h.shape[0], h.shape[1]
    z = jnp.einsum('btd,dn->btn', h, lp['w_in'])
    offs = np.cumsum(IN_SIZES)[:-1].tolist()
    a_b, a_c, a_x, p_in, q, kv, g_nsa, g_mrg = jnp.split(z, offs, axis=-1)
    v, conv_state = short_conv(a_c * a_x, conv_prefix, lp['conv_w'])
    y_a = a_b * v
    y_b, pool_state = pool_mixer(p_in, pool_prefix, pos, lp['pool_w'], lp['pool_scale'])
    q = q.reshape(B, T, N_KV, Q_PER_KV, HEAD_DIM)
    kv = kv.reshape(B, T, 3, 2, N_KV, HEAD_DIM)
    g_nsa = jax.nn.sigmoid(g_nsa.reshape(B, T, N_KV, Q_PER_KV, 3))
    y_c, nsa_state = nsa_fn(q, kv, g_nsa)
    g = jax.nn.sigmoid(g_mrg.reshape(B, T, 3, D_MODEL))
    wb = lp['w_branch']
    merged = (g[:, :, 0] * (y_a @ wb[:D_CONV])
              + g[:, :, 1] * (y_b @ wb[D_CONV:D_CONV + D_POOL])
              + g[:, :, 2] * (y_c @ wb[D_CONV + D_POOL:]))
    return merged @ lp['w_out'], conv_state, pool_state, nsa_state


def trunk_layer(x, c, lp, conv_prefix, pool_prefix, pos, nsa_fn):
    mod = jnp.einsum('bd,dn->bn', jax.nn.silu(c), lp['ada_w']) + lp['ada_b']
    sh1, sc1, ga1, sh2, sc2, ga2 = jnp.split(mod[:, None, :], 6, axis=-1)
    ng = lp['norm_g']
    h = rmsnorm(x, ng[0]) * (1 + sc1) + sh1
    m, conv_s, pool_s, nsa_s = mixer_sublayer(h, lp, conv_prefix, pool_prefix, pos, nsa_fn)
    x = x + ga1 * rmsnorm(m, ng[1])
    h = rmsnorm(x, ng[2]) * (1 + sc2) + sh2
    f = jnp.einsum('btf,fd->btd', jnp.square(jax.nn.relu(jnp.einsum('btd,df->btf', h, lp['mlp_w1']))), lp['mlp_w2'])
    x = x + ga2 * rmsnorm(f, ng[3])
    return x, conv_s, pool_s, nsa_s


def setup_inputs(seed: int = 0) -> dict:
    key = jax.random.key(seed)
    k = jax.random.split(key, 26)
    n_pages = PAST_LEN // PAGE_SIZE
    n_used = DEC_BATCH * n_pages
    n_phys = n_used + max(1, n_used // 4)
    win_buf = min(WINDOW, PAST_LEN)

    def nrm(kk, shape, s=None):
        a = jax.random.normal(kk, shape, jnp.float32)
        return a if s is None else a * s

    page_table = jax.random.permutation(k[4], n_phys)[:n_used].reshape(DEC_BATCH, n_pages).astype(jnp.int32)
    return {
        'x_prompt': nrm(k[0], (BATCH, SEQ, D_MODEL)),
        'x_sample': nrm(k[1], (DEC_BATCH, DEC_SEQ, D_MODEL)),
        'cache_cmp_kv': nrm(k[2], (DEPTH, n_phys, PAGE_SIZE, 2, N_KV, HEAD_DIM)),
        'cache_slc_kv': nrm(k[3], (DEPTH, n_phys, PAGE_SIZE, 2, N_KV, HEAD_DIM)),
        'state_win_kv': nrm(k[5], (DEPTH, DEC_BATCH, win_buf, 2, N_KV, HEAD_DIM)),
        'state_conv': nrm(k[6], (DEPTH, DEC_BATCH, CONV_W - 1, D_CONV)),
        'state_pool': nrm(k[7], (DEPTH, DEC_BATCH, POOL_STATE, D_POOL)),
        'page_table': page_table,
        'c_prompt': nrm(k[8], (BATCH, D_MODEL)),
        'c_sample': nrm(k[9], (DEC_BATCH, D_MODEL)),
        'rel_bias': nrm(k[10], (N_BUCKETS, N_HEADS), 0.5),
        'norm_g': 1.0 + nrm(k[11], (DEPTH, 4, D_MODEL), 0.05),
        'ada_w': nrm(k[12], (DEPTH, D_MODEL, 6 * D_MODEL), D_MODEL ** -0.5),
        'ada_b': nrm(k[13], (DEPTH, 6 * D_MODEL), 0.02),
        'w_in': nrm(k[14], (DEPTH, D_MODEL, N_IN), D_MODEL ** -0.5),
        'conv_w': nrm(k[15], (DEPTH, CONV_W, D_CONV), CONV_W ** -0.5),
        'pool_w': nrm(k[16], (DEPTH, N_POOL_GROUPS, POOL_GROUP, POOL_GROUP), POOL_GROUP ** -0.5),
        'pool_scale': 1.0 + nrm(k[17], (DEPTH, D_POOL), 0.05),
        'cmp_pe': nrm(k[18], (DEPTH, 2, CMP_BLOCK, HEAD_DIM), 0.1),
        'cmp_w1': nrm(k[19], (DEPTH, 2, CMP_BLOCK * HEAD_DIM, CMP_HIDDEN), (CMP_BLOCK * HEAD_DIM) ** -0.5),
        'cmp_w2': nrm(k[20], (DEPTH, 2, CMP_HIDDEN, HEAD_DIM), CMP_HIDDEN ** -0.5),
        'w_branch': nrm(k[21], (DEPTH, D_MIX, D_MODEL), D_CONV ** -0.5),
        'w_out': nrm(k[22], (DEPTH, D_MODEL, D_MODEL), D_MODEL ** -0.5),
        'mlp_w1': nrm(k[23], (DEPTH, D_MODEL, D_FF), D_MODEL ** -0.5),
        'mlp_w2': nrm(k[24], (DEPTH, D_FF, D_MODEL), D_FF ** -0.5),
    }


def reference(x_prompt, x_sample, cache_cmp_kv, cache_slc_kv, state_win_kv, state_conv, state_pool, page_table,
              c_prompt, c_sample, rel_bias, norm_g, ada_w, ada_b, w_in, conv_w, pool_w, pool_scale,
              cmp_pe, cmp_w1, cmp_w2, w_branch, w_out, mlp_w1, mlp_w2):
    T = x_prompt.shape[1]
    S = x_sample.shape[1]
    pos_p = jnp.arange(T, dtype=jnp.int32)
    pos_s = PAST_LEN + jnp.arange(S, dtype=jnp.int32)
    conv0 = jnp.zeros((x_prompt.shape[0], CONV_W - 1, D_CONV), x_prompt.dtype)
    pool0 = jnp.zeros((x_prompt.shape[0], POOL_STATE, D_POOL), x_prompt.dtype)
    xp, xs = x_prompt, x_sample
    cmp_p, cmp_s, slc_p, slc_s, win_p, win_s, conv_p, conv_s, pool_p, pool_s = ([] for _ in range(10))
    for l in range(DEPTH):
        lp = {'norm_g': norm_g[l], 'ada_w': ada_w[l], 'ada_b': ada_b[l], 'w_in': w_in[l], 'conv_w': conv_w[l],
              'pool_w': pool_w[l], 'pool_scale': pool_scale[l], 'w_branch': w_branch[l], 'w_out': w_out[l],
              'mlp_w1': mlp_w1[l], 'mlp_w2': mlp_w2[l]}
        nsa_p = functools.partial(nsa_prompt, cmp_pe[l], cmp_w1[l], cmp_w2[l], rel_bias)
        xp, cst, pst, (crow, srow, wbuf) = trunk_layer(xp, c_prompt, lp, conv0, pool0, pos_p, nsa_p)
        cmp_p.append(crow); slc_p.append(srow); win_p.append(wbuf); conv_p.append(cst); pool_p.append(pst)
        nsa_s = functools.partial(nsa_sample, cmp_pe[l], cmp_w1[l], cmp_w2[l], rel_bias,
                                  cache_cmp_kv[l], cache_slc_kv[l], page_table, state_win_kv[l])
        xs, cst, pst, (crow, srow, wbuf) = trunk_layer(xs, c_sample, lp, state_conv[l], state_pool[l], pos_s, nsa_s)
        cmp_s.append(crow); slc_s.append(srow); win_s.append(wbuf); conv_s.append(cst); pool_s.append(pst)
    return (xp, xs,
            jnp.stack(cmp_p), jnp.stack(cmp_s),
            jnp.stack(slc_p), jnp.stack(slc_s),
            jnp.stack(win_p), jnp.stack(win_s),
            jnp.stack(conv_p), jnp.stack(conv_s),
            jnp.stack(pool_p), jnp.stack(pool_s))
```

```python
import functools
import math

import numpy as np
import jax
import jax.numpy as jnp
from jax import lax
from jax.experimental import pallas as pl
from jax.experimental.pallas import tpu as pltpu

F32 = jnp.float32
BF16 = jnp.bfloat16

D_MODEL = 2048
DEPTH = 4
PAST_LEN = 16384
PAGE = 128
D_CONV = 1024
D_POOL = 1024
POOL_WINDOWS = (2, 4, 8, 16)
POOL_GROUP = 256
HEAD_DIM = 128
N_KV = 4
Q_PER_KV = 4
N_HEADS = 16
CMP_STRIDE = 16
SEL_BLOCK = 64
N_SEL = 16
WINDOW = 512
N_BUCKETS = 32
MAX_DISTANCE = 128
D_FF = 8192
RMS_EPS = 1e-6
NEG_BIG = -1e9
POS_BIG = 1e9
SCALE = HEAD_DIM ** -0.5

NEG = -1e30
KV_PAD = 1024
VMEM_LIMIT = 56 * 1024 * 1024
TQ = 64


def _cparams(*sem):
    return pltpu.CompilerParams(dimension_semantics=sem, vmem_limit_bytes=VMEM_LIMIT)


def _dot(a, b):
    return jnp.dot(a, b, preferred_element_type=F32)


def _dot_nt(a, b):
    return lax.dot_general(a, b, (((1,), (1,)), ((), ())), preferred_element_type=F32)


def _split3(x):
    h1 = x.astype(BF16)
    r1 = x - h1.astype(F32)
    h2 = r1.astype(BF16)
    h3 = (r1 - h2.astype(F32)).astype(BF16)
    return h1, h2, h3


def _mm_kernel(a_ref, w_ref, o_ref, *scratch, nk, act):
    part = _dot(a_ref[...].astype(BF16), w_ref[...])

    def finish(acc):
        if act == "relu2":
            r = jnp.maximum(acc, 0.0)
            acc = r * r
        o_ref[...] = acc.astype(o_ref.dtype)

    if nk == 1:
        finish(part)
    else:
        acc_ref = scratch[0]
        k = pl.program_id(2)

        @pl.when(k == 0)
        def _():
            acc_ref[...] = part

        @pl.when(k > 0)
        def _():
            acc_ref[...] += part

        @pl.when(k == nk - 1)
        def _():
            finish(acc_ref[...])


def _mm(a, w, layer, col0, n, out_dtype, act=None, tn=512):
    m, k = a.shape
    tm = min(m, 1024)
    tk = min(k, 2048)
    tn = min(tn, n)
    nk = k // tk
    assert m % tm == 0 and k % tk == 0 and n % tn == 0 and col0 % tn == 0
    cb0 = col0 // tn
    return pl.pallas_call(
        functools.partial(_mm_kernel, nk=nk, act=act),
        grid=(n // tn, m // tm, nk),
        in_specs=[
            pl.BlockSpec((tm, tk), lambda j, i, kk: (i, kk)),
            pl.BlockSpec((None, tk, tn), lambda j, i, kk: (layer, kk, cb0 + j)),
        ],
        out_specs=pl.BlockSpec((tm, tn), lambda j, i, kk: (i, j)),
        out_shape=jax.ShapeDtypeStruct((m, n), out_dtype),
        scratch_shapes=[pltpu.VMEM((tm, tn), F32)] if nk > 1 else [],
        compiler_params=_cparams("parallel", "parallel", "arbitrary"),
    )(a, w)


def _merge_kernel(ya_ref, yb_ref, yc_ref, g0_ref, g1_ref, g2_ref, wa_ref, wb_ref, wc_ref, o_ref):
    acc = jax.nn.sigmoid(g0_ref[...]) * _dot(ya_ref[...].astype(BF16), wa_ref[...])
    acc += jax.nn.sigmoid(g1_ref[...]) * _dot(yb_ref[...].astype(BF16), wb_ref[...])
    acc += jax.nn.sigmoid(g2_ref[...]) * _dot(yc_ref[...].astype(BF16), wc_ref[...])
    o_ref[...] = acc.astype(o_ref.dtype)


def _merge(ya, yb, yc, gm, w_branch, layer):
    m = ya.shape[0]
    tm = min(m, 1024)
    tn = 512
    nb = D_MODEL // tn
    return pl.pallas_call(
        _merge_kernel,
        grid=(nb, m // tm),
        in_specs=[
            pl.BlockSpec((tm, D_CONV), lambda j, i: (i, 0)),
            pl.BlockSpec((tm, D_POOL), lambda j, i: (i, 0)),
            pl.BlockSpec((tm, D_MODEL), lambda j, i: (i, 0)),
            pl.BlockSpec((tm, tn), lambda j, i: (i, j)),
            pl.BlockSpec((tm, tn), lambda j, i: (i, nb + j)),
            pl.BlockSpec((tm, tn), lambda j, i: (i, 2 * nb + j)),
            pl.BlockSpec((None, D_CONV, tn), lambda j, i: (layer, 0, j)),
            pl.BlockSpec((None, D_POOL, tn), lambda j, i: (layer, 1, j)),
            pl.BlockSpec((None, D_MODEL, tn), lambda j, i: (layer, 1, j)),
        ],
        out_specs=pl.BlockSpec((tm, tn), lambda j, i: (i, j)),
        out_shape=jax.ShapeDtypeStruct((m, D_MODEL), BF16),
        compiler_params=_cparams("parallel", "parallel"),
    )(ya, yb, yc, gm, gm, gm, w_branch, w_branch, w_branch)


def _ada_kernel(c_ref, w_ref, b_ref, o_ref):
    c = c_ref[...]
    a = (c * jax.nn.sigmoid(c)).astype(BF16)
    o_ref[...] = _dot(a, w_ref[...]) + b_ref[...]


def _ada(c16, ada_w, ada_b):
    tn = 2048
    n = 6 * D_MODEL
    return pl.pallas_call(
        _ada_kernel,
        grid=(DEPTH, n // tn),
        in_specs=[
            pl.BlockSpec((16, D_MODEL), lambda l, j: (0, 0)),
            pl.BlockSpec((None, D_MODEL, tn), lambda l, j: (l, 0, j)),
            pl.BlockSpec((None, 1, tn), lambda l, j: (l, 0, j)),
        ],
        out_specs=pl.BlockSpec((None, 16, tn), lambda l, j: (l, 0, j)),
        out_shape=jax.ShapeDtypeStruct((DEPTH, 16, n), F32),
        compiler_params=_cparams("parallel", "parallel"),
    )(c16, ada_w, ada_b.reshape(DEPTH, 1, n))


def _rms(x, g):
    return x * lax.rsqrt(jnp.mean(x * x, axis=-1, keepdims=True) + RMS_EPS) * g


def _norm_kernel(*refs, res, hh):
    it = iter(refs)
    x = next(it)[0]
    if res is not None:
        m_ref, ng_ref, mod_ref = next(it), next(it), next(it)
    if hh is not None:
        ngh_ref, modh_ref = next(it), next(it)
    if res is not None:
        _, ng_row, gate_row = res
        x = x + mod_ref[0, gate_row:gate_row + 1, :] * _rms(m_ref[0], ng_ref[ng_row:ng_row + 1, :])
        next(it)[0] = x
    if hh is not None:
        _, ng_row, shift_row, scale_row = hh
        h = (_rms(x, ngh_ref[ng_row:ng_row + 1, :]) * (1.0 + modh_ref[0, scale_row:scale_row + 1, :])
             + modh_ref[0, shift_row:shift_row + 1, :])
        next(it)[0] = h.astype(BF16)


def _norm(x, m, norm_g, mod, res, hh):
    b, t, d = x.shape
    tr = min(t, 512)
    xspec = pl.BlockSpec((1, tr, d), lambda bb, i: (bb, i, 0))

    def params(layer):
        return ([norm_g, mod],
                [pl.BlockSpec((None, 4, d), lambda bb, i: (layer, 0, 0)),
                 pl.BlockSpec((None, 1, 6, d), lambda bb, i: (layer, bb, 0, 0))])

    ins, specs, outs, ospecs = [x], [xspec], [], []
    if res is not None:
        a, s = params(res[0])
        ins += [m] + a
        specs += [xspec] + s
        outs.append(jax.ShapeDtypeStruct((b, t, d), F32))
        ospecs.append(xspec)
    if hh is not None:
        a, s = params(hh[0])
        ins += a
        specs += s
        outs.append(jax.ShapeDtypeStruct((b, t, d), BF16))
        ospecs.append(xspec)
    out = pl.pallas_call(
        functools.partial(_norm_kernel, res=res, hh=hh),
        grid=(b, t // tr),
        in_specs=specs,
        out_specs=ospecs,
        out_shape=outs,
        compiler_params=_cparams("parallel", "parallel"),
    )(*ins)
    out = list(out)
    x_new = out.pop(0) if res is not None else None
    h = out.pop(0) if hh is not None else None
    return x_new, h


def _convpool_kernel(ab_ref, ac_ref, ax_ref, p_ref, cpre_ref, ppre_ref, cw_ref, pw_ref, ps_ref,
                     ya_ref, yb_ref, cst_ref, pst_ref, u_ext, p_ext, *, tr, nt, pos0):
    t = pl.program_id(1)

    @pl.when(t == 0)
    def _():
        u_ext[0:8, :] = jnp.zeros((8, D_CONV), F32)
        u_ext[6:8, :] = cpre_ref[0]
        p_ext[0:16, :] = jnp.zeros((16, D_POOL), F32)
        p_ext[1:16, :] = ppre_ref[0]

    u = ac_ref[0] * ax_ref[0]
    u_ext[8:8 + tr, :] = u
    v = cw_ref[0:1, :] * u_ext[6:6 + tr, :] + cw_ref[1:2, :] * u_ext[7:7 + tr, :] + cw_ref[2:3, :] * u
    ya_ref[0] = (ab_ref[0] * v).astype(ya_ref.dtype)

    p = p_ref[0]
    p_ext[16:16 + tr, :] = p
    pos = pos0 + t * tr + lax.broadcasted_iota(jnp.int32, (tr, 1), 0)
    for g, w in enumerate(POOL_WINDOWS):
        lo = g * POOL_GROUP
        s = p[:, lo:lo + POOL_GROUP]
        for k in range(1, w):
            s = s + p_ext[16 - k:16 - k + tr, lo:lo + POOL_GROUP]
        cnt = jnp.minimum(pos + 1, w).astype(F32)
        dlt = s / cnt - p[:, lo:lo + POOL_GROUP]
        y = _dot(dlt.astype(BF16), pw_ref[g])
        yb_ref[0, :, lo:lo + POOL_GROUP] = (y * ps_ref[:, lo:lo + POOL_GROUP]).astype(yb_ref.dtype)

    @pl.when(t == nt - 1)
    def _():
        cst_ref[0] = u_ext[tr + 6:tr + 8, :]
        pst_ref[0] = p_ext[tr + 1:tr + 16, :]

    if nt > 1:
        u_ext[0:8, :] = u_ext[tr:tr + 8, :]
        p_ext[0:16, :] = p_ext[tr:tr + 16, :]


def _convpool(za, conv_pre, pool_pre, conv_w, pool_w, pool_scale, layer, pos0):
    b, t, _ = za.shape
    tr = min(t, 512)
    nt = t // tr

    def col(c):
        return pl.BlockSpec((1, tr, 1024), lambda bb, i: (bb, i, c))

    return pl.pallas_call(
        functools.partial(_convpool_kernel, tr=tr, nt=nt, pos0=pos0),
        grid=(b, nt),
        in_specs=[col(0), col(1), col(2), col(3),
                  pl.BlockSpec((1, 2, D_CONV), lambda bb, i: (bb, 0, 0)),
                  pl.BlockSpec((1, 15, D_POOL), lambda bb, i: (bb, 0, 0)),
                  pl.BlockSpec((None, 3, D_CONV), lambda bb, i: (layer, 0, 0)),
                  pl.BlockSpec((None, 4, POOL_GROUP, POOL_GROUP), lambda bb, i: (layer, 0, 0, 0)),
                  pl.BlockSpec((None, 1, D_POOL), lambda bb, i: (layer, 0, 0))],
        out_specs=[pl.BlockSpec((1, tr, D_CONV), lambda bb, i: (bb, i, 0)),
                   pl.BlockSpec((1, tr, D_POOL), lambda bb, i: (bb, i, 0)),
                   pl.BlockSpec((1, 2, D_CONV), lambda bb, i: (bb, 0, 0)),
                   pl.BlockSpec((1, 15, D_POOL), lambda bb, i: (bb, 0, 0))],
        out_shape=[jax.ShapeDtypeStruct((b, t, D_CONV), BF16),
                   jax.ShapeDtypeStruct((b, t, D_POOL), BF16),
                   jax.ShapeDtypeStruct((b, 2, D_CONV), F32),
                   jax.ShapeDtypeStruct((b, 15, D_POOL), F32)],
        scratch_shapes=[pltpu.VMEM((tr + 8, D_CONV), F32), pltpu.VMEM((tr + 16, D_POOL), F32)],
        compiler_params=_cparams("parallel", "arbitrary"),
    )(za, za, za, za, conv_pre, pool_pre, conv_w, pool_w, pool_scale.reshape(DEPTH, 1, D_POOL))


def _compress_kernel(pt_ref, *refs, n_in, cpr, nsteps, n16):
    del pt_ref
    x_refs = refs[:n_in]
    w1_ref, pe_ref, w2_ref, kc_ref, vc_ref, ab_ref = refs[n_in:]
    s = pl.program_id(1)
    tc = cpr * n_in
    row_w = 2 * N_KV * HEAD_DIM

    @pl.when(s == 0)
    def _():
        ab_ref[:, n16:n16 + 8, :] = jnp.zeros((8, 8, 256), F32)

    for y in range(2):
        parts = []
        for g in range(N_KV):
            c0 = (y * N_KV + g) * HEAD_DIM
            for xr in x_refs:
                parts.append(jnp.concatenate(
                    [xr[0, :, p * row_w + c0:p * row_w + c0 + HEAD_DIM] for p in range(CMP_STRIDE)], axis=1))
        xcat = jnp.concatenate(parts, axis=0).astype(BF16)
        ab = _dot(xcat, w1_ref[y])
        for g in range(N_KV):
            ab_ref[y * N_KV + g, pl.ds(pl.multiple_of(s * tc, 8), tc), :] = ab[g * tc:(g + 1) * tc]

    @pl.when(s == nsteps - 1)
    def _():
        for y in range(2):
            pe = jnp.broadcast_to(pe_ref[y], (8, 2 * CMP_STRIDE * HEAD_DIM)).astype(BF16)
            half = CMP_STRIDE * HEAD_DIM
            c0 = (_dot(pe[:, :half], w1_ref[y, :, 0:HEAD_DIM])
                  + _dot(pe[:, half:], w1_ref[y, :, HEAD_DIM:2 * HEAD_DIM]))[0:1]
            for g in range(N_KV):
                yg = y * N_KV + g
                hid = ab_ref[yg, 0:n16, 0:HEAD_DIM] + ab_ref[yg, 1:n16 + 1, HEAD_DIM:2 * HEAD_DIM] + c0
                o = _dot(jax.nn.gelu(hid).astype(BF16), w2_ref[y])
                if y == 0:
                    kc_ref[0, g] = o.astype(kc_ref.dtype)
                else:
                    vc_ref[0, g] = o.astype(vc_ref.dtype)


def _compress(x_list, x_specs, pt, nb, nsteps, cpr, w1ab, pe, w2, layer):
    n_in = len(x_list)
    n16 = nsteps * n_in * cpr
    grid_spec = pltpu.PrefetchScalarGridSpec(
        num_scalar_prefetch=1,
        grid=(nb, nsteps),
        in_specs=list(x_specs) + [
            pl.BlockSpec((None, 2, 2048, 256), lambda b, s, p: (layer, 0, 0, 0)),
            pl.BlockSpec((None, 2, 1, 4096), lambda b, s, p: (layer, 0, 0, 0)),
            pl.BlockSpec((None, 2, HEAD_DIM, HEAD_DIM), lambda b, s, p: (layer, 0, 0, 0)),
        ],
        out_specs=[pl.BlockSpec((1, N_KV, n16, HEAD_DIM), lambda b, s, p: (b, 0, 0, 0)),
                   pl.BlockSpec((1, N_KV, n16, HEAD_DIM), lambda b, s, p: (b, 0, 0, 0))],
        scratch_shapes=[pltpu.VMEM((8, n16 + 8, 256), F32)],
    )
    return pl.pallas_call(
        functools.partial(_compress_kernel, n_in=n_in, cpr=cpr, nsteps=nsteps, n16=n16),
        grid_spec=grid_spec,
        out_shape=[jax.ShapeDtypeStruct((nb, N_KV, n16, HEAD_DIM), BF16),
                   jax.ShapeDtypeStruct((nb, N_KV, n16, HEAD_DIM), BF16)],
        compiler_params=_cparams("parallel", "arbitrary"),
    )(pt, *x_list, w1ab, pe, w2)


def _bucket_np(d):
    n = np.maximum(d, 0)
    exact = N_BUCKETS // 2
    nf = np.maximum(n, 1).astype(np.float64)
    frac = np.log(nf / exact) / math.log(MAX_DISTANCE / exact) * (N_BUCKETS - exact)
    near_int = np.abs(frac - np.round(frac)) < 1e-6
    assert not np.any(near_int & (n > exact) & (n < MAX_DISTANCE))
    large = exact + np.floor(frac + 1e-9).astype(np.int64)
    return np.where(n < exact, n, np.minimum(large, N_BUCKETS - 1)).astype(np.int32)


def _idx_np(d, ok):
    return np.where(ok, _bucket_np(d), -1).astype(np.int32)


def _table_kernel(rb_ref, idx_ref, o_ref):
    idx = idx_ref[...]
    for h in range(N_HEADS):
        far = rb_ref[N_BUCKETS - 1, h]
        acc = jnp.zeros(idx.shape, F32)
        for bkt in range(N_BUCKETS - 1):
            acc = jnp.where(idx == bkt, rb_ref[bkt, h] - far, acc)
        o_ref[h] = jnp.where(idx < 0, NEG, acc)


def _bias_table(rel_bias, idx):
    r, c = idx.shape
    return pl.pallas_call(
        _table_kernel,
        in_specs=[pl.BlockSpec(memory_space=pltpu.SMEM), pl.BlockSpec((r, c), lambda: (0, 0))],
        out_specs=pl.BlockSpec((N_HEADS, r, c), lambda: (0, 0, 0)),
        out_shape=jax.ShapeDtypeStruct((N_HEADS, r, c), F32),
    )(rel_bias, jnp.asarray(idx))


def _prompt_tables(rel_bias):
    t = np.arange(TQ)[:, None]
    k = np.arange(256)[None, :]
    d = 64 * (3 - k // 64) + t - (k % 64)
    btn = _bias_table(rel_bias, _idx_np(d, d >= 0))
    d = 16 * (12 - k) + t - 31
    cb = _bias_table(rel_bias, np.where(k < 16, _idx_np(d, d >= 0), N_BUCKETS - 1).astype(np.int32))
    g = N_KV
    return btn.reshape(g, Q_PER_KV * TQ, 256), cb.reshape(g, Q_PER_KV * TQ, 256)


def _window_mid_mask_np():
    t = np.arange(TQ)[:, None]
    k = np.arange(512)[None, :]
    d = 64 * (11 - k // 64) + t - (k % 64)
    m = np.where(d < WINDOW, 0.0, NEG).astype(np.float32)
    return np.tile(m, (Q_PER_KV, 1))


def _sel_matrix_np(n_slc_pad, nc_pad, nc):
    j = np.arange(n_slc_pad)[:, None]
    c = np.arange(nc_pad)[None, :]
    return ((c >= 4 * j - 1) & (c <= 4 * j + 3) & (c < nc)).astype(np.float32)


def _online(s, v, m, l, acc):
    m_new = jnp.maximum(m, jnp.max(s, axis=1, keepdims=True))
    a = jnp.exp(m - m_new)
    p = jnp.exp(s - m_new)
    l = a * l + jnp.sum(p, axis=1, keepdims=True)
    acc = a * acc + _dot(p.astype(BF16), v)
    return m_new, l, acc


def _first(s, v):
    m = jnp.max(s, axis=1, keepdims=True)
    p = jnp.exp(s - m)
    return m, jnp.sum(p, axis=1, keepdims=True), _dot(p.astype(BF16), v)


def _rank_select(sc, n_rows, jj, cur):
    cnt = jnp.zeros(sc.shape, F32)
    for j2 in range(n_rows):
        row = sc[j2:j2 + 1, :]
        beats = (row > sc) | ((row == sc) & (jj > j2))
        cnt = cnt + jnp.where(beats, 1.0, 0.0)
    return jnp.where((cnt < N_SEL - 0.5) & (jj <= cur), 1.0, 0.0)


def _attn_prompt_kernel(q_ref, gt_ref, kc_ref, vc_ref, ks_ref, vs_ref, kw_ref, vw_ref,
                        btn_ref, cb_ref, pm_ref, st_ref, o_ref):
    i = pl.program_id(2)
    rows = Q_PER_KV * TQ
    q = q_ref[0]
    qg = jnp.concatenate([q[:, r * HEAD_DIM:(r + 1) * HEAD_DIM] for r in range(Q_PER_KV)], axis=0)

    s = _dot_nt(qg, kc_ref[0, 0]) * SCALE
    lane = lax.broadcasted_iota(jnp.int32, (rows, 256), 1)
    cbr = pltpu.roll(cb_ref[0], lax.rem(4 * i + 244, 256), 1)
    s = s + jnp.where(lane >= 4 * i + 4, NEG, cbr)
    ok = s > 0.5 * NEG
    mx = jnp.max(s, axis=1, keepdims=True)
    mx = jnp.where(mx > 0.5 * NEG, mx, 0.0)
    e = jnp.where(ok, jnp.exp(s - mx), 0.0)
    p = e * (1.0 / jnp.maximum(jnp.sum(e, axis=1, keepdims=True), 1e-30))
    o_c = _dot(p.astype(BF16), vc_ref[0, 0])

    imp = p[0:TQ] + p[TQ:2 * TQ] + p[2 * TQ:3 * TQ] + p[3 * TQ:4 * TQ]
    st = st_ref[...]
    h1, h2, h3 = _split3(imp)
    ps_t = _dot_nt(st, h1) + _dot_nt(st, h2) + _dot_nt(st, h3)
    jj = lax.broadcasted_iota(jnp.int32, (64, TQ), 0)
    forced = (jj == 0) | (jj == i) | (jj == i - 1)
    sc = jnp.where(jj > i, NEG_BIG, jnp.where(forced, POS_BIG, ps_t))
    sel_t = _rank_select(sc, 64, jj, i)
    eye = jnp.where(lax.broadcasted_iota(jnp.int32, (TQ, TQ), 0) == lax.broadcasted_iota(jnp.int32, (TQ, TQ), 1),
                    1.0, 0.0).astype(BF16)
    sel = _dot_nt(eye, sel_t.astype(BF16))
    sel_rows = jnp.concatenate([sel] * Q_PER_KV, axis=0).astype(BF16)

    btn = btn_ref[0]
    base_n = pl.multiple_of(KV_PAD + (i - 3) * 64, 64)
    kidx = lax.broadcasted_iota(jnp.int32, (rows, 256), 1)

    j2 = lax.broadcasted_iota(jnp.int32, (64, 256), 0)
    kb2 = lax.shift_right_logical(lax.broadcasted_iota(jnp.int32, (64, 256), 1), 6)
    e_near = jnp.where(j2 == kb2 + (i - 3), 1.0, 0.0).astype(BF16)
    s = _dot_nt(qg, ks_ref[0, pl.ds(base_n, 256), :]) * SCALE + btn
    s = jnp.where(_dot(sel_rows, e_near) > 0.5, s, NEG)
    m_s, l_s, a_s = _first(s, vs_ref[0, pl.ds(base_n, 256), :])

    j3 = lax.broadcasted_iota(jnp.int32, (64, 512), 0)
    kb3 = lax.shift_right_logical(lax.broadcasted_iota(jnp.int32, (64, 512), 1), 6)

    def far_body(c, carry):
        base = pl.multiple_of(KV_PAD + c * 512, 512)
        e_far = jnp.where((j3 == kb3 + 8 * c) & (j3 <= i - 4), 1.0, 0.0).astype(BF16)
        sf = _dot_nt(qg, ks_ref[0, pl.ds(base, 512), :]) * SCALE
        sf = jnp.where(_dot(sel_rows, e_far) > 0.5, sf, NEG)
        return _online(sf, vs_ref[0, pl.ds(base, 512), :], *carry)

    n_far = jnp.where(i >= 4, (i - 4) // 8 + 1, 0)
    m_s, l_s, a_s = lax.fori_loop(0, n_far, far_body, (m_s, l_s, a_s))
    o_s = a_s * (1.0 / l_s)

    s = _dot_nt(qg, kw_ref[0, pl.ds(base_n, 256), :]) * SCALE + btn
    s = jnp.where(kidx >= (3 - i) * 64, s, NEG)
    m_w, l_w, a_w = _first(s, vw_ref[0, pl.ds(base_n, 256), :])
    base_m = pl.multiple_of(KV_PAD + (i - 11) * 64, 64)
    kidx5 = lax.broadcasted_iota(jnp.int32, (rows, 512), 1)
    s = _dot_nt(qg, kw_ref[0, pl.ds(base_m, 512), :]) * SCALE + pm_ref[...]
    s = jnp.where(kidx5 >= (11 - i) * 64, s, NEG)
    m_w, l_w, a_w = _online(s, vw_ref[0, pl.ds(base_m, 512), :], m_w, l_w, a_w)
    o_w = a_w * (1.0 / l_w)

    gt = jax.nn.sigmoid(gt_ref[0, 0])

    def gcol(br):
        return jnp.concatenate([gt[:, 3 * r + br:3 * r + br + 1] for r in range(Q_PER_KV)], axis=0)

    out = gcol(0) * o_c + gcol(1) * o_s + gcol(2) * o_w
    for r in range(Q_PER_KV):
        o_ref[0, :, r * HEAD_DIM:(r + 1) * HEAD_DIM] = out[r * TQ:(r + 1) * TQ].astype(o_ref.dtype)


def _attn_prompt(q, gt, kc, vc, ks, kw, btn, cb, pm, st):
    b, t, _ = q.shape
    lp = ks.shape[1]
    nc = kc.shape[2]
    assert nc == 256 and t // SEL_BLOCK == 64
    kspec = pl.BlockSpec((1, lp, HEAD_DIM), lambda bb, g, i: (bb, 0, g))
    vspec = pl.BlockSpec((1, lp, HEAD_DIM), lambda bb, g, i: (bb, 0, N_KV + g))
    return pl.pallas_call(
        _attn_prompt_kernel,
        grid=(b, N_KV, t // TQ),
        in_specs=[
            pl.BlockSpec((1, TQ, Q_PER_KV * HEAD_DIM), lambda bb, g, i: (bb, i, g)),
            pl.BlockSpec((1, 1, TQ, 12), lambda bb, g, i: (bb, g, i, 0)),
            pl.BlockSpec((1, 1, nc, HEAD_DIM), lambda bb, g, i: (bb, g, 0, 0)),
            pl.BlockSpec((1, 1, nc, HEAD_DIM), lambda bb, g, i: (bb, g, 0, 0)),
            kspec, vspec, kspec, vspec,
            pl.BlockSpec((1, Q_PER_KV * TQ, 256), lambda bb, g, i: (g, 0, 0)),
            pl.BlockSpec((1, Q_PER_KV * TQ, 256), lambda bb, g, i: (g, 0, 0)),
            pl.BlockSpec((Q_PER_KV * TQ, 512), lambda bb, g, i: (0, 0)),
            pl.BlockSpec((64, 256), lambda bb, g, i: (0, 0)),
        ],
        out_specs=pl.BlockSpec((1, TQ, Q_PER_KV * HEAD_DIM), lambda bb, g, i: (bb, i, g)),
        out_shape=jax.ShapeDtypeStruct((b, t, N_HEADS * HEAD_DIM), BF16),
        compiler_params=_cparams("parallel", "parallel", "arbitrary"),
    )(q, gt, kc, vc, ks, ks, kw, kw, btn, cb, pm, st)


S_DEC = 8
S_ROWS = N_HEADS * S_DEC
PAGES_PER_STEP = 8
N_PAGES = PAST_LEN // PAGE
N_STEPS = N_PAGES // PAGES_PER_STEP
NC_S = PAST_LEN // CMP_STRIDE
CUR_S = PAST_LEN // SEL_BLOCK
NJ_S = 264
GW = N_KV * HEAD_DIM


def _diag_blocks(x):
    return jnp.concatenate([x[g * 32:(g + 1) * 32, g * HEAD_DIM:(g + 1) * HEAD_DIM] for g in range(N_KV)], axis=0)


def _attn_sample_kernel(pt_ref, q_ref, gt_ref, kc_ref, vc_ref, *refs):
    del pt_ref
    pages = refs[:PAGES_PER_STEP]
    (snew_ref, wbuf_ref, wnew_ref, tc_ref, tl_ref, tn_ref, tw_ref, st_ref, e16_ref,
     o_ref, qb_ref, sc_ref, sel_ref, m_ref, l_ref, acc_ref, oc_ref) = refs[PAGES_PER_STEP:]
    step = pl.program_id(1)

    @pl.when(step == 0)
    def _():
        q = q_ref[0].astype(F32)
        zero = jnp.zeros((32, HEAD_DIM), F32)
        qgs, qb_rows = [], []
        for g in range(N_KV):
            qg = jnp.concatenate([q[:, (4 * g + r) * HEAD_DIM:(4 * g + r + 1) * HEAD_DIM]
                                  for r in range(Q_PER_KV)], axis=0)
            qgs.append(qg.astype(BF16))
            qb_rows.append(jnp.concatenate([qg if gg == g else zero for gg in range(N_KV)], axis=1))
        qb = jnp.concatenate(qb_rows, axis=0).astype(BF16)
        qb_ref[...] = qb

        s = jnp.concatenate([_dot_nt(qgs[g], kc_ref[0, g]) for g in range(N_KV)], axis=0) * SCALE + tc_ref[...]
        ok = s > 0.5 * NEG
        mx = jnp.max(s, axis=1, keepdims=True)
        mx = jnp.where(mx > 0.5 * NEG, mx, 0.0)
        e = jnp.where(ok, jnp.exp(s - mx), 0.0)
        p = e * (1.0 / jnp.maximum(jnp.sum(e, axis=1, keepdims=True), 1e-30))
        pb = p.astype(BF16)
        oc_ref[...] = jnp.concatenate([_dot(pb[g * 32:(g + 1) * 32], vc_ref[0, g]) for g in range(N_KV)], axis=0)

        reps = []
        for g in range(N_KV):
            pg = p[g * 32:(g + 1) * 32]
            ig = pg[0:8] + pg[8:16] + pg[16:24] + pg[24:32]
            reps += [ig] * Q_PER_KV
        imp = jnp.concatenate(reps, axis=0)
        st = st_ref[...]
        h1, h2, h3 = _split3(imp)
        ps_t = _dot_nt(st, h1) + _dot_nt(st, h2) + _dot_nt(st, h3)
        jj = lax.broadcasted_iota(jnp.int32, (NJ_S, S_ROWS), 0)
        forced = (jj == 0) | (jj == CUR_S) | (jj == CUR_S - 1)
        sc = jnp.where(jj > CUR_S, NEG_BIG, jnp.where(forced, POS_BIG, ps_t))
        sc_ref[...] = sc

        def rank_body(j2, cnt):
            row = sc_ref[pl.ds(j2, 1), :]
            beats = (row > sc) | ((row == sc) & (jj > j2))
            return cnt + jnp.where(beats, 1.0, 0.0)

        cnt = lax.fori_loop(0, NJ_S, rank_body, jnp.zeros((NJ_S, S_ROWS), F32))
        sel_ref[...] = jnp.where((cnt < N_SEL - 0.5) & (jj <= CUR_S), 1.0, 0.0)

        new = jnp.concatenate([snew_ref[0], jnp.zeros((128 - S_DEC, 2 * GW), F32)], axis=0).astype(BF16)
        sn = _dot_nt(qb, new[:, :GW]) * SCALE + tn_ref[...]
        m0, l0, a0 = _first(sn, new[:, GW:])
        m_ref[...] = m0
        l_ref[...] = l0
        acc_ref[...] = a0

    x = jnp.concatenate([pg[0] for pg in pages], axis=0)
    kb = x[:, :GW].astype(BF16)
    vb = x[:, GW:].astype(BF16)
    qb = qb_ref[...]
    s = _dot_nt(qb, kb) * SCALE + jnp.where(step == N_STEPS - 1, 1.0, 0.0) * tl_ref[...]
    blocks = PAGES_PER_STEP * PAGE // SEL_BLOCK
    sel_step = sel_ref[pl.ds(pl.multiple_of(step * blocks, blocks), blocks), :]
    eye = jnp.where(lax.broadcasted_iota(jnp.int32, (S_ROWS, S_ROWS), 0)
                    == lax.broadcasted_iota(jnp.int32, (S_ROWS, S_ROWS), 1), 1.0, 0.0).astype(BF16)
    sel_r = _dot_nt(eye, sel_step.astype(BF16)).astype(BF16)
    s = jnp.where(_dot(sel_r, e16_ref[...]) > 0.5, s, NEG)
    m1, l1, a1 = _online(s, vb, m_ref[...], l_ref[...], acc_ref[...])
    m_ref[...] = m1
    l_ref[...] = l1
    acc_ref[...] = a1

    @pl.when(step == N_STEPS - 1)
    def _():
        o_s = _diag_blocks(a1 * (1.0 / l1))
        w = jnp.concatenate([wbuf_ref[0], wnew_ref[0], jnp.zeros((128 - S_DEC, 2 * GW), F32)], axis=0).astype(BF16)
        sw = _dot_nt(qb, w[:, :GW]) * SCALE + tw_ref[...]
        _, lw, aw = _first(sw, w[:, GW:])
        o_w = _diag_blocks(aw * (1.0 / lw))
        gt = jax.nn.sigmoid(gt_ref[0])
        out = gt[:, 0:1] * oc_ref[...] + gt[:, 1:2] * o_s + gt[:, 2:3] * o_w
        for h in range(N_HEADS):
            o_ref[0, :, h * HEAD_DIM:(h + 1) * HEAD_DIM] = out[h * S_DEC:(h + 1) * S_DEC]


def _attn_sample(q, gt, kc, vc, cache, page_table, page_base, snew, wbuf, wnew, layer, tabs):
    b = q.shape[0]
    tc, tl, tn, tw, st, e16 = tabs

    def page_spec(j):
        return pl.BlockSpec((1, PAGE, 2 * GW),
                            lambda bb, s, pt: (page_base + pt[bb, s * PAGES_PER_STEP + j], 0, 0))

    def full(shape):
        return pl.BlockSpec(shape, lambda bb, s, pt: tuple(0 for _ in shape))

    grid_spec = pltpu.PrefetchScalarGridSpec(
        num_scalar_prefetch=1,
        grid=(b, N_STEPS),
        in_specs=[
            pl.BlockSpec((1, S_DEC, N_HEADS * HEAD_DIM), lambda bb, s, pt: (bb, 0, 0)),
            pl.BlockSpec((1, S_ROWS, 3), lambda bb, s, pt: (bb, 0, 0)),
            pl.BlockSpec((1, N_KV, NC_S, HEAD_DIM), lambda bb, s, pt: (bb, 0, 0, 0)),
            pl.BlockSpec((1, N_KV, NC_S, HEAD_DIM), lambda bb, s, pt: (bb, 0, 0, 0)),
        ] + [page_spec(j) for j in range(PAGES_PER_STEP)] + [
            pl.BlockSpec((1, S_DEC, 2 * GW), lambda bb, s, pt: (bb, 0, 0)),
            pl.BlockSpec((None, 1, WINDOW, 2 * GW), lambda bb, s, pt: (layer, bb, 0, 0)),
            pl.BlockSpec((1, S_DEC, 2 * GW), lambda bb, s, pt: (bb, 0, 1)),
            full(tc.shape), full(tl.shape), full(tn.shape), full(tw.shape), full(st.shape), full(e16.shape),
        ],
        out_specs=pl.BlockSpec((1, S_DEC, N_HEADS * HEAD_DIM), lambda bb, s, pt: (bb, 0, 0)),
        scratch_shapes=[pltpu.VMEM((S_ROWS, GW), BF16), pltpu.VMEM((NJ_S, S_ROWS), F32),
                        pltpu.VMEM((NJ_S, S_ROWS), F32), pltpu.VMEM((S_ROWS, 1), F32), pltpu.VMEM((S_ROWS, 1), F32),
                        pltpu.VMEM((S_ROWS, GW), F32), pltpu.VMEM((S_ROWS, HEAD_DIM), F32)],
    )
    return pl.pallas_call(
        _attn_sample_kernel,
        grid_spec=grid_spec,
        out_shape=jax.ShapeDtypeStruct((b, S_DEC, N_HEADS * HEAD_DIM), F32),
        compiler_params=_cparams("parallel", "arbitrary"),
    )(page_table, q, gt, kc, vc, *([cache] * PAGES_PER_STEP), snew, wbuf, wnew, tc, tl, tn, tw, st, e16)


def _sample_tables(rel_bias):
    s = np.arange(S_DEC)[:, None]
    qpos = PAST_LEN + s

    def rows(tab):
        return tab.reshape(S_ROWS, tab.shape[-1])

    c = np.arange(NC_S)[None, :]
    d = qpos - (CMP_STRIDE * c + 2 * CMP_STRIDE - 1)
    tc = rows(_bias_table(rel_bias, _idx_np(d, d >= 0)))
    k = np.arange(PAGES_PER_STEP * PAGE)[None, :]
    d = qpos - (PAST_LEN - PAGES_PER_STEP * PAGE + k)
    tl = rows(_bias_table(rel_bias, _idx_np(d, d >= 0)))
    k = np.arange(128)[None, :]
    d = s - k
    tn = rows(_bias_table(rel_bias, _idx_np(d, (d >= 0) & (k < S_DEC))))
    k = np.arange(WINDOW + 128)[None, :]
    d = np.where(k < WINDOW, qpos - (PAST_LEN - WINDOW + k), s - (k - WINDOW))
    tw = rows(_bias_table(rel_bias, _idx_np(d, (d >= 0) & (d < WINDOW) & (k < WINDOW + S_DEC))))
    st = jnp.asarray(_sel_matrix_np(NJ_S, NC_S, NC_S - 1), BF16)
    blocks = PAGES_PER_STEP * PAGE // SEL_BLOCK
    e16 = jnp.asarray((np.arange(blocks)[:, None] == np.arange(PAGES_PER_STEP * PAGE)[None, :] // SEL_BLOCK)
                      .astype(np.float32), BF16)
    return tc, tl, tn, tw, st, e16


def _projections(hf, w_main, w_gn, w_gm, layer):
    za = _mm(hf, w_main, layer, 0, 4096, F32)
    q = _mm(hf, w_main, layer, 4096, 2048, BF16)
    cmp = _mm(hf, w_main, layer, 6144, 1024, F32)
    sw = _mm(hf, w_main, layer, 7168, 2048, F32)
    gn = _mm(hf, w_gn, layer, 0, 48, F32)
    gm = _mm(hf, w_gm, layer, 0, 3 * D_MODEL, F32)
    return za, q, cmp, sw, gn, gm


def kernel(x_prompt, x_sample, cache_cmp_kv, cache_slc_kv, state_win_kv, state_conv, state_pool, page_table,
           c_prompt, c_sample, rel_bias, norm_g, ada_w, ada_b, w_in, conv_w, pool_w, pool_scale,
           cmp_pe, cmp_w1, cmp_w2, w_branch, w_out, mlp_w1, mlp_w2):
    bp, t, d = x_prompt.shape
    bs, s_len, _ = x_sample.shape
    n_phys = cache_cmp_kv.shape[1]
    assert (t, d, s_len, bs) == (4096, D_MODEL, S_DEC, 8)

    n_main = 9216
    w_main = w_in[:, :, :n_main].astype(BF16)
    w_gn = w_in[:, :, n_main:n_main + 48].astype(BF16)
    w_gm = w_in[:, :, n_main + 48:].astype(BF16)
    ada_wb = ada_w.astype(BF16)
    w_branch_b = w_branch.astype(BF16)
    w_out_b = w_out.astype(BF16)
    w1_b = mlp_w1.astype(BF16)
    w2_b = mlp_w2.astype(BF16)
    pool_wb = pool_w.astype(BF16)
    w1r = cmp_w1.reshape(DEPTH, 2, 2 * CMP_STRIDE, HEAD_DIM, HEAD_DIM)
    half = CMP_STRIDE * HEAD_DIM
    w1ab = jnp.concatenate([w1r[:, :, :CMP_STRIDE].reshape(DEPTH, 2, half, HEAD_DIM),
                            w1r[:, :, CMP_STRIDE:].reshape(DEPTH, 2, half, HEAD_DIM)], axis=-1).astype(BF16)
    pe = cmp_pe.reshape(DEPTH, 2, 1, 2 * half)
    w2c = cmp_w2.astype(BF16)

    c16 = jnp.concatenate([c_prompt, c_sample, jnp.zeros((16 - bp - bs, d), F32)], axis=0)
    mod = _ada(c16, ada_wb, ada_b)
    mod_p = mod[:, :bp].reshape(DEPTH, bp, 6, d)
    mod_s = mod[:, bp:bp + bs].reshape(DEPTH, bs, 6, d)

    btn, cb = _prompt_tables(rel_bias)
    pm = jnp.asarray(_window_mid_mask_np())
    st_p = jnp.asarray(_sel_matrix_np(t // SEL_BLOCK, t // CMP_STRIDE, t // CMP_STRIDE - 1), BF16)
    tabs_s = _sample_tables(rel_bias)

    cache_c = cache_cmp_kv.reshape(DEPTH * n_phys, PAGE // CMP_STRIDE, CMP_STRIDE * 2 * GW)
    cache_s = cache_slc_kv.reshape(DEPTH * n_phys, PAGE, 2 * GW)
    wbuf = state_win_kv.reshape(DEPTH, bs, WINDOW, 2 * GW)
    conv0 = jnp.zeros((bp, 2, D_CONV), F32)
    pool0 = jnp.zeros((bp, 15, D_POOL), F32)
    dummy_pt = jnp.zeros((1, 1), jnp.int32)

    xp, xs = x_prompt, x_sample
    _, hp = _norm(xp, None, norm_g, mod_p, None, (0, 0, 0, 1))
    _, hs = _norm(xs, None, norm_g, mod_s, None, (0, 0, 0, 1))
    outs = [[] for _ in range(10)]
    mp, ms = bp * t, bs * s_len
    for l in range(DEPTH):
        za, q, cmp, sw, gn, gm = _projections(hp.reshape(mp, d), w_main, w_gn, w_gm, l)
        ya, yb, cst_p, pst_p = _convpool(za.reshape(bp, t, 4096), conv0, pool0, conv_w, pool_wb, pool_scale, l, 0)
        n16 = t // CMP_STRIDE
        kc, vc = _compress([cmp.reshape(bp, n16, CMP_STRIDE * 2 * GW)],
                           [pl.BlockSpec((1, 64, CMP_STRIDE * 2 * GW), lambda b, s, p: (b, s, 0))],
                           dummy_pt, bp, n16 // 64, 64, w1ab, pe, w2c, l)
        swp = sw.reshape(bp, t, 4 * GW)
        ks = jnp.pad(swp[:, :, :2 * GW].astype(BF16), ((0, 0), (KV_PAD, 0), (0, 0)))
        kw = jnp.pad(swp[:, :, 2 * GW:].astype(BF16), ((0, 0), (KV_PAD, 0), (0, 0)))
        gt = gn.reshape(bp, t, N_KV, 12).transpose(0, 2, 1, 3)
        yc = _attn_prompt(q.reshape(bp, t, 2048), gt, kc, vc, ks, kw, btn, cb, pm, st_p)
        merged = _merge(ya.reshape(mp, D_CONV), yb.reshape(mp, D_POOL), yc.reshape(mp, 2048), gm, w_branch_b, l)
        mo = _mm(merged, w_out_b, l, 0, d, F32)
        xp, h2 = _norm(xp, mo.reshape(bp, t, d), norm_g, mod_p, (l, 1, 2), (l, 2, 3, 4))
        u = _mm(h2.reshape(mp, d), w1_b, l, 0, D_FF, BF16, act="relu2")
        f = _mm(u, w2_b, l, 0, d, F32)
        xp, hp = _norm(xp, f.reshape(bp, t, d), norm_g, mod_p, (l, 3, 5), (l + 1, 0, 0, 1) if l + 1 < DEPTH else None)
        sw5 = swp.reshape(bp, t, 2, 2, N_KV, HEAD_DIM)
        outs[0].append(cmp.reshape(bp, t, 2, N_KV, HEAD_DIM))
        outs[2].append(sw5[:, :, 0])
        outs[4].append(sw5[:, t - WINDOW:, 1])
        outs[6].append(cst_p)
        outs[8].append(pst_p)

        za, q, cmp, sw, gn, gm = _projections(hs.reshape(ms, d), w_main, w_gn, w_gm, l)
        ya, yb, cst_s, pst_s = _convpool(za.reshape(bs, s_len, 4096), state_conv[l], state_pool[l],
                                         conv_w, pool_wb, pool_scale, l, PAST_LEN)
        sws = sw.reshape(bs, s_len, 4 * GW)
        cpp = PAGE // CMP_STRIDE
        page_specs = [pl.BlockSpec((1, cpp, CMP_STRIDE * 2 * GW),
                                   lambda b, s, p, j=j, l=l: (l * n_phys + p[b, s * PAGES_PER_STEP + j], 0, 0))
                      for j in range(PAGES_PER_STEP)]
        kc, vc = _compress([cache_c] * PAGES_PER_STEP, page_specs, page_table, bs, N_STEPS, cpp, w1ab, pe, w2c, l)
        gts = gn.reshape(bs, s_len, N_KV, Q_PER_KV, 3).transpose(0, 2, 3, 1, 4).reshape(bs, S_ROWS, 3)
        yc = _attn_sample(q.reshape(bs, s_len, 2048), gts, kc, vc, cache_s, page_table, l * n_phys,
                          sws, wbuf, sws, l, tabs_s)
        merged = _merge(ya.reshape(ms, D_CONV), yb.reshape(ms, D_POOL), yc.reshape(ms, 2048), gm, w_branch_b, l)
        mo = _mm(merged, w_out_b, l, 0, d, F32)
        xs, h2 = _norm(xs, mo.reshape(bs, s_len, d), norm_g, mod_s, (l, 1, 2), (l, 2, 3, 4))
        u = _mm(h2.reshape(ms, d), w1_b, l, 0, D_FF, BF16, act="relu2")
        f = _mm(u, w2_b, l, 0, d, F32)
        xs, hs = _norm(xs, f.reshape(bs, s_len, d), norm_g, mod_s, (l, 3, 5), (l + 1, 0, 0, 1) if l + 1 < DEPTH else None)
        sw5 = sws.reshape(bs, s_len, 2, 2, N_KV, HEAD_DIM)
        outs[1].append(cmp.reshape(bs, s_len, 2, N_KV, HEAD_DIM))
        outs[3].append(sw5[:, :, 0])
        outs[5].append(jnp.concatenate([state_win_kv[l][:, s_len:], sw5[:, :, 1]], axis=1))
        outs[7].append(cst_s)
        outs[9].append(pst_s)

    return (xp, xs) + tuple(jnp.stack(o) for o in outs)
```

```python
import functools
import math

import numpy as np
import jax
import jax.numpy as jnp
from jax import lax
from jax.experimental import pallas as pl
from jax.experimental.pallas import tpu as pltpu

F32 = jnp.float32
BF16 = jnp.bfloat16

D_MODEL = 2048
DEPTH = 4
PAST_LEN = 16384
PAGE = 128
D_CONV = 1024
D_POOL = 1024
POOL_WINDOWS = (2, 4, 8, 16)
POOL_GROUP = 256
HEAD_DIM = 128
N_KV = 4
Q_PER_KV = 4
N_HEADS = 16
CMP_STRIDE = 16
SEL_BLOCK = 64
N_SEL = 16
WINDOW = 512
N_BUCKETS = 32
MAX_DISTANCE = 128
D_FF = 8192
RMS_EPS = 1e-6
NEG_BIG = -1e9
POS_BIG = 1e9
SCALE = HEAD_DIM ** -0.5
LOG2E = math.log2(math.e)
C1 = SCALE * LOG2E

NEG = -1e30
KV_PAD = 1024
VMEM_LIMIT = 56 * 1024 * 1024
TQ = 64


def _cparams(*sem):
    return pltpu.CompilerParams(dimension_semantics=sem, vmem_limit_bytes=VMEM_LIMIT)


def _dot(a, b):
    return jnp.dot(a, b, preferred_element_type=F32)


def _dot_nt(a, b):
    return lax.dot_general(a, b, (((1,), (1,)), ((), ())), preferred_element_type=F32)


def _split3(x):
    h1 = x.astype(BF16)
    r1 = x - h1.astype(F32)
    h2 = r1.astype(BF16)
    h3 = (r1 - h2.astype(F32)).astype(BF16)
    return h1, h2, h3


def _mm_kernel(a_ref, w_ref, o_ref, *scratch, nk, act, stage_w, w_t):
    if stage_w:
        wb_ref = scratch[-1]

        @pl.when(pl.program_id(1) == 0)
        def _():
            wb_ref[...] = w_ref[...].astype(BF16)

        w = wb_ref[...]
    else:
        w = w_ref[...].astype(BF16)
    part = (_dot_nt if w_t else _dot)(a_ref[...].astype(BF16), w)

    def finish(acc):
        if act == "relu2":
            r = jnp.maximum(acc, 0.0)
            acc = r * r
        o_ref[...] = acc.astype(o_ref.dtype)

    if nk == 1:
        finish(part)
    else:
        acc_ref = scratch[0]
        k = pl.program_id(2)

        @pl.when(k == 0)
        def _():
            acc_ref[...] = part

        @pl.when(k > 0)
        def _():
            acc_ref[...] += part

        @pl.when(k == nk - 1)
        def _():
            finish(acc_ref[...])


def _mm(a, w, layer, col0, n, out_dtype, act=None, tn=512, w_t=False):
    m, k = a.shape
    tm = min(m, 1024)
    tk = min(k, 2048)
    tn = min(tn, n)
    nk = k // tk
    assert m % tm == 0 and k % tk == 0 and n % tn == 0 and col0 % tn == 0
    cb0 = col0 // tn
    stage_w = nk == 1 and m // tm > 1 and w.dtype != BF16
    wblock = (tn, tk) if w_t else (tk, tn)
    scratch = ([pltpu.VMEM((tm, tn), F32)] if nk > 1 else []) + ([pltpu.VMEM(wblock, BF16)] if stage_w else [])
    if w_t:
        wspec = pl.BlockSpec((None, tn, tk), lambda j, i, kk: (layer, cb0 + j, kk))
    else:
        wspec = pl.BlockSpec((None, tk, tn), lambda j, i, kk: (layer, kk, cb0 + j))
    return pl.pallas_call(
        functools.partial(_mm_kernel, nk=nk, act=act, stage_w=stage_w, w_t=w_t),
        name="mm",
        grid=(n // tn, m // tm, nk),
        in_specs=[pl.BlockSpec((tm, tk), lambda j, i, kk: (i, kk)), wspec],
        out_specs=pl.BlockSpec((tm, tn), lambda j, i, kk: (i, j)),
        out_shape=jax.ShapeDtypeStruct((m, n), out_dtype),
        scratch_shapes=scratch,
        compiler_params=_cparams("parallel", "arbitrary", "arbitrary"),
    )(a, w)


def _merge_kernel(ya_ref, yb_ref, yc_ref, g0_ref, g1_ref, g2_ref, wa_ref, wb_ref, wc_ref, o_ref):
    acc = jax.nn.sigmoid(g0_ref[...]) * _dot(ya_ref[...].astype(BF16), wa_ref[...])
    acc += jax.nn.sigmoid(g1_ref[...]) * _dot(yb_ref[...].astype(BF16), wb_ref[...])
    acc += jax.nn.sigmoid(g2_ref[...]) * _dot(yc_ref[...].astype(BF16), wc_ref[...])
    o_ref[...] = acc.astype(o_ref.dtype)


def _merge(ya, yb, yc, gm, w_branch, layer):
    m = ya.shape[0]
    tm = min(m, 1024)
    tn = 512
    nb = D_MODEL // tn
    return pl.pallas_call(
        _merge_kernel,
        name="merge",
        grid=(nb, m // tm),
        in_specs=[
            pl.BlockSpec((tm, D_CONV), lambda j, i: (i, 0)),
            pl.BlockSpec((tm, D_POOL), lambda j, i: (i, 0)),
            pl.BlockSpec((tm, D_MODEL), lambda j, i: (i, 0)),
            pl.BlockSpec((tm, tn), lambda j, i: (i, j)),
            pl.BlockSpec((tm, tn), lambda j, i: (i, nb + j)),
            pl.BlockSpec((tm, tn), lambda j, i: (i, 2 * nb + j)),
            pl.BlockSpec((None, D_CONV, tn), lambda j, i: (layer, 0, j)),
            pl.BlockSpec((None, D_POOL, tn), lambda j, i: (layer, 1, j)),
            pl.BlockSpec((None, D_MODEL, tn), lambda j, i: (layer, 1, j)),
        ],
        out_specs=pl.BlockSpec((tm, tn), lambda j, i: (i, j)),
        out_shape=jax.ShapeDtypeStruct((m, D_MODEL), BF16),
        compiler_params=_cparams("parallel", "parallel"),
    )(ya, yb, yc, gm, gm, gm, w_branch, w_branch, w_branch)


def _ada_kernel(c_ref, w_ref, b_ref, o_ref):
    c = c_ref[...]
    a = (c * jax.nn.sigmoid(c)).astype(BF16)
    o_ref[...] = _dot(a, w_ref[...].astype(BF16)) + b_ref[...]


def _ada(c16, ada_w, ada_b):
    tn = 1024
    n = 6 * D_MODEL
    return pl.pallas_call(
        _ada_kernel,
        name="ada",
        grid=(DEPTH, n // tn),
        in_specs=[
            pl.BlockSpec((16, D_MODEL), lambda l, j: (0, 0)),
            pl.BlockSpec((None, D_MODEL, tn), lambda l, j: (l, 0, j)),
            pl.BlockSpec((None, 1, tn), lambda l, j: (l, 0, j)),
        ],
        out_specs=pl.BlockSpec((None, 16, tn), lambda l, j: (l, 0, j)),
        out_shape=jax.ShapeDtypeStruct((DEPTH, 16, n), F32),
        compiler_params=_cparams("parallel", "parallel"),
    )(c16, ada_w, ada_b.reshape(DEPTH, 1, n))


def _rms(x, g):
    return x * lax.rsqrt(jnp.mean(x * x, axis=-1, keepdims=True) + RMS_EPS) * g


def _norm_kernel(*refs, res, hh):
    it = iter(refs)
    x = next(it)[0]
    if res is not None:
        m_ref, ng_ref, mod_ref = next(it), next(it), next(it)
    if hh is not None:
        ngh_ref, modh_ref = next(it), next(it)
    if res is not None:
        _, ng_row, gate_row = res
        x = x + mod_ref[0, gate_row:gate_row + 1, :] * _rms(m_ref[0], ng_ref[ng_row:ng_row + 1, :])
        next(it)[0] = x
    if hh is not None:
        _, ng_row, shift_row, scale_row = hh
        h = (_rms(x, ngh_ref[ng_row:ng_row + 1, :]) * (1.0 + modh_ref[0, scale_row:scale_row + 1, :])
             + modh_ref[0, shift_row:shift_row + 1, :])
        next(it)[0] = h.astype(BF16)


def _norm(x, m, norm_g, mod, res, hh):
    b, t, d = x.shape
    tr = min(t, 512)
    xspec = pl.BlockSpec((1, tr, d), lambda bb, i: (bb, i, 0))

    def params(layer):
        return ([norm_g, mod],
                [pl.BlockSpec((None, 4, d), lambda bb, i: (layer, 0, 0)),
                 pl.BlockSpec((None, 1, 6, d), lambda bb, i: (layer, bb, 0, 0))])

    ins, specs, outs, ospecs = [x], [xspec], [], []
    if res is not None:
        a, s = params(res[0])
        ins += [m] + a
        specs += [xspec] + s
        outs.append(jax.ShapeDtypeStruct((b, t, d), F32))
        ospecs.append(xspec)
    if hh is not None:
        a, s = params(hh[0])
        ins += a
        specs += s
        outs.append(jax.ShapeDtypeStruct((b, t, d), BF16))
        ospecs.append(xspec)
    out = pl.pallas_call(
        functools.partial(_norm_kernel, res=res, hh=hh),
        name="norm",
        grid=(b, t // tr),
        in_specs=specs,
        out_specs=ospecs,
        out_shape=outs,
        compiler_params=_cparams("parallel", "parallel"),
    )(*ins)
    out = list(out)
    x_new = out.pop(0) if res is not None else None
    h = out.pop(0) if hh is not None else None
    return x_new, h


def _convpool_kernel(ab_ref, ac_ref, ax_ref, p_ref, cpre_ref, ppre_ref, cw_ref, pw_ref, ps_ref,
                     ya_ref, yb_ref, cst_ref, pst_ref, u_ext, p_ext, *, tr, nt, pos0):
    t = pl.program_id(1)

    @pl.when(t == 0)
    def _():
        u_ext[0:8, :] = jnp.zeros((8, D_CONV), F32)
        u_ext[6:8, :] = cpre_ref[0]
        p_ext[0:16, :] = jnp.zeros((16, D_POOL), F32)
        p_ext[1:16, :] = ppre_ref[0]

    u = ac_ref[0] * ax_ref[0]
    u_ext[8:8 + tr, :] = u
    v = cw_ref[0:1, :] * u_ext[6:6 + tr, :] + cw_ref[1:2, :] * u_ext[7:7 + tr, :] + cw_ref[2:3, :] * u
    ya_ref[0] = (ab_ref[0] * v).astype(ya_ref.dtype)

    p = p_ref[0]
    p_ext[16:16 + tr, :] = p
    pos = pos0 + t * tr + lax.broadcasted_iota(jnp.int32, (tr, 1), 0)
    for g, w in enumerate(POOL_WINDOWS):
        lo = g * POOL_GROUP
        s = p[:, lo:lo + POOL_GROUP]
        for k in range(1, w):
            s = s + p_ext[16 - k:16 - k + tr, lo:lo + POOL_GROUP]
        cnt = jnp.minimum(pos + 1, w).astype(F32)
        dlt = s / cnt - p[:, lo:lo + POOL_GROUP]
        y = _dot(dlt.astype(BF16), pw_ref[g])
        yb_ref[0, :, lo:lo + POOL_GROUP] = (y * ps_ref[:, lo:lo + POOL_GROUP]).astype(yb_ref.dtype)

    @pl.when(t == nt - 1)
    def _():
        cst_ref[0] = u_ext[tr + 6:tr + 8, :]
        pst_ref[0] = p_ext[tr + 1:tr + 16, :]

    if nt > 1:
        u_ext[0:8, :] = u_ext[tr:tr + 8, :]
        p_ext[0:16, :] = p_ext[tr:tr + 16, :]


def _convpool(za, conv_pre, pool_pre, conv_w, pool_w, pool_scale, layer, pos0):
    b, t, _ = za.shape
    tr = min(t, 512)
    nt = t // tr

    def col(c):
        return pl.BlockSpec((1, tr, 1024), lambda bb, i: (bb, i, c))

    return pl.pallas_call(
        functools.partial(_convpool_kernel, tr=tr, nt=nt, pos0=pos0),
        name="convpool",
        grid=(b, nt),
        in_specs=[col(0), col(1), col(2), col(3),
                  pl.BlockSpec((1, 2, D_CONV), lambda bb, i: (bb, 0, 0)),
                  pl.BlockSpec((1, 15, D_POOL), lambda bb, i: (bb, 0, 0)),
                  pl.BlockSpec((None, 3, D_CONV), lambda bb, i: (layer, 0, 0)),
                  pl.BlockSpec((None, 4, POOL_GROUP, POOL_GROUP), lambda bb, i: (layer, 0, 0, 0)),
                  pl.BlockSpec((None, 1, D_POOL), lambda bb, i: (layer, 0, 0))],
        out_specs=[pl.BlockSpec((1, tr, D_CONV), lambda bb, i: (bb, i, 0)),
                   pl.BlockSpec((1, tr, D_POOL), lambda bb, i: (bb, i, 0)),
                   pl.BlockSpec((1, 2, D_CONV), lambda bb, i: (bb, 0, 0)),
                   pl.BlockSpec((1, 15, D_POOL), lambda bb, i: (bb, 0, 0))],
        out_shape=[jax.ShapeDtypeStruct((b, t, D_CONV), BF16),
                   jax.ShapeDtypeStruct((b, t, D_POOL), BF16),
                   jax.ShapeDtypeStruct((b, 2, D_CONV), F32),
                   jax.ShapeDtypeStruct((b, 15, D_POOL), F32)],
        scratch_shapes=[pltpu.VMEM((tr + 8, D_CONV), F32), pltpu.VMEM((tr + 16, D_POOL), F32)],
        compiler_params=_cparams("parallel", "arbitrary"),
    )(za, za, za, za, conv_pre, pool_pre, conv_w, pool_w, pool_scale.reshape(DEPTH, 1, D_POOL))


def _compress_kernel(pt_ref, *refs, n_in, cpr, nsteps, n16, paged):
    del pt_ref
    x_refs = refs[:n_in]
    w1_ref, pe_ref, w2_ref, kc_ref, vc_ref, ab_ref = refs[n_in:]
    s = pl.program_id(1)
    tc = cpr * n_in
    row_w = 2 * N_KV * HEAD_DIM

    @pl.when(s == 0)
    def _():
        ab_ref[:, n16:n16 + 8, :] = jnp.zeros((8, 8, 256), F32)

    for y in range(2):
        parts = []
        for g in range(N_KV):
            c0 = (y * N_KV + g) * HEAD_DIM
            for xr in x_refs:
                if paged:
                    tok = [xr[0, pl.ds(p * 2 * N_KV + y * N_KV + g, cpr, stride=CMP_STRIDE * 2 * N_KV), :]
                           for p in range(CMP_STRIDE)]
                else:
                    tok = [xr[0, :, p * row_w + c0:p * row_w + c0 + HEAD_DIM] for p in range(CMP_STRIDE)]
                parts.append(jnp.concatenate(tok, axis=1))
        xcat = jnp.concatenate(parts, axis=0).astype(BF16)
        ab = _dot(xcat, w1_ref[y])
        for g in range(N_KV):
            ab_ref[y * N_KV + g, pl.ds(pl.multiple_of(s * tc, 8), tc), :] = ab[g * tc:(g + 1) * tc]

    @pl.when(s == nsteps - 1)
    def _():
        for y in range(2):
            pe = jnp.broadcast_to(pe_ref[y], (8, 2 * CMP_STRIDE * HEAD_DIM)).astype(BF16)
            half = CMP_STRIDE * HEAD_DIM
            c0 = (_dot(pe[:, :half], w1_ref[y, :, 0:HEAD_DIM])
                  + _dot(pe[:, half:], w1_ref[y, :, HEAD_DIM:2 * HEAD_DIM]))[0:1]
            for g in range(N_KV):
                yg = y * N_KV + g
                hid = ab_ref[yg, 0:n16, 0:HEAD_DIM] + ab_ref[yg, 1:n16 + 1, HEAD_DIM:2 * HEAD_DIM] + c0
                o = _dot(jax.nn.gelu(hid).astype(BF16), w2_ref[y])
                if y == 0:
                    kc_ref[0, g] = o.astype(kc_ref.dtype)
                else:
                    vc_ref[0, g] = o.astype(vc_ref.dtype)


def _compress(x_list, x_specs, pt, nb, nsteps, cpr, w1ab, pe, w2, layer, paged=False):
    n_in = len(x_list)
    n16 = nsteps * n_in * cpr
    grid_spec = pltpu.PrefetchScalarGridSpec(
        num_scalar_prefetch=1,
        grid=(nb, nsteps),
        in_specs=list(x_specs) + [
            pl.BlockSpec((None, 2, 2048, 256), lambda b, s, p: (layer, 0, 0, 0)),
            pl.BlockSpec((None, 2, 1, 4096), lambda b, s, p: (layer, 0, 0, 0)),
            pl.BlockSpec((None, 2, HEAD_DIM, HEAD_DIM), lambda b, s, p: (layer, 0, 0, 0)),
        ],
        out_specs=[pl.BlockSpec((1, N_KV, n16, HEAD_DIM), lambda b, s, p: (b, 0, 0, 0)),
                   pl.BlockSpec((1, N_KV, n16, HEAD_DIM), lambda b, s, p: (b, 0, 0, 0))],
        scratch_shapes=[pltpu.VMEM((8, n16 + 8, 256), F32)],
    )
    return pl.pallas_call(
        functools.partial(_compress_kernel, n_in=n_in, cpr=cpr, nsteps=nsteps, n16=n16, paged=paged),
        name="compress_paged" if paged else "compress",
        grid_spec=grid_spec,
        out_shape=[jax.ShapeDtypeStruct((nb, N_KV, n16, HEAD_DIM), BF16),
                   jax.ShapeDtypeStruct((nb, N_KV, n16, HEAD_DIM), BF16)],
        compiler_params=_cparams("parallel", "arbitrary"),
    )(pt, *x_list, w1ab, pe, w2)


def _bucket_np(d):
    n = np.maximum(d, 0)
    exact = N_BUCKETS // 2
    nf = np.maximum(n, 1).astype(np.float64)
    frac = np.log(nf / exact) / math.log(MAX_DISTANCE / exact) * (N_BUCKETS - exact)
    near_int = np.abs(frac - np.round(frac)) < 1e-6
    assert not np.any(near_int & (n > exact) & (n < MAX_DISTANCE))
    large = exact + np.floor(frac + 1e-9).astype(np.int64)
    return np.where(n < exact, n, np.minimum(large, N_BUCKETS - 1)).astype(np.int32)


def _idx_np(d, ok):
    return np.where(ok, _bucket_np(d), -1).astype(np.int32)


def _table_kernel(rb_ref, idx_ref, o_ref):
    idx = idx_ref[...]
    for h in range(N_HEADS):
        far = rb_ref[N_BUCKETS - 1, h]
        acc = jnp.zeros(idx.shape, F32)
        for bkt in range(N_BUCKETS - 1):
            acc = jnp.where(idx == bkt, (rb_ref[bkt, h] - far) * LOG2E, acc)
        o_ref[h] = jnp.where(idx < 0, NEG, acc)


def _bias_table(rel_bias, idx):
    r, c = idx.shape
    return pl.pallas_call(
        _table_kernel,
        name="bias_table",
        in_specs=[pl.BlockSpec(memory_space=pltpu.SMEM), pl.BlockSpec((r, c), lambda: (0, 0))],
        out_specs=pl.BlockSpec((N_HEADS, r, c), lambda: (0, 0, 0)),
        out_shape=jax.ShapeDtypeStruct((N_HEADS, r, c), F32),
    )(rel_bias, jnp.asarray(idx))


def _prompt_tables(rel_bias):
    t = np.arange(TQ)[:, None]
    k = np.arange(256)[None, :]
    d = 64 * (3 - k // 64) + t - (k % 64)
    btn = _bias_table(rel_bias, _idx_np(d, d >= 0))
    d = 16 * (12 - k) + t - 31
    cb = _bias_table(rel_bias, np.where(k < 16, _idx_np(d, d >= 0), N_BUCKETS - 1).astype(np.int32))
    g = N_KV
    return btn.reshape(g, Q_PER_KV * TQ, 256), cb.reshape(g, Q_PER_KV * TQ, 256)


def _window_mid_mask_np():
    t = np.arange(TQ)[:, None]
    k = np.arange(512)[None, :]
    d = 64 * (11 - k // 64) + t - (k % 64)
    m = np.where(d < WINDOW, 0.0, NEG).astype(np.float32)
    return np.tile(m, (Q_PER_KV, 1))


def _sel_matrix_np(n_slc_pad, nc_pad, nc):
    j = np.arange(n_slc_pad)[:, None]
    c = np.arange(nc_pad)[None, :]
    return ((c >= 4 * j - 1) & (c <= 4 * j + 3) & (c < nc)).astype(np.float32)


def _online(s, v, m, l, acc):
    m_new = jnp.maximum(m, jnp.max(s, axis=1, keepdims=True))
    a = jnp.exp2(m - m_new)
    p = jnp.exp2(s - m_new)
    l = a * l + jnp.sum(p, axis=1, keepdims=True)
    acc = a * acc + _dot(p.astype(BF16), v)
    return m_new, l, acc


def _first(s, v):
    m = jnp.max(s, axis=1, keepdims=True)
    p = jnp.exp2(s - m)
    return m, jnp.sum(p, axis=1, keepdims=True), _dot(p.astype(BF16), v)


def _online_aug(s, v_aug, m, acc):
    m_new = jnp.maximum(m, jnp.max(s, axis=1, keepdims=True))
    acc = jnp.exp2(m - m_new) * acc + _dot(jnp.exp2(s - m_new).astype(BF16), v_aug)
    return m_new, acc


def _first_aug(s, v_aug):
    m = jnp.max(s, axis=1, keepdims=True)
    return m, _dot(jnp.exp2(s - m).astype(BF16), v_aug)


def _normalise_aug(acc):
    return acc[:, :HEAD_DIM] * (1.0 / acc[:, HEAD_DIM:])


def _rank_select(sc, n_rows, jj, cur):
    cnt = jnp.zeros(sc.shape, F32)
    for j2 in range(n_rows):
        row = sc[j2:j2 + 1, :]
        beats = (row > sc) | ((row == sc) & (jj > j2))
        cnt = cnt + jnp.where(beats, 1.0, 0.0)
    return jnp.where((cnt < N_SEL - 0.5) & (jj <= cur), 1.0, 0.0)


def _attn_prompt_kernel(q_ref, gt_ref, kc_ref, vc_ref, ks_ref, vs_ref, kw_ref, vw_ref,
                        btn_ref, cb_ref, pm_ref, st_ref, et_ref, pad_ref, one_ref, o_ref):
    i = pl.program_id(2)
    rows = Q_PER_KV * TQ
    q = q_ref[0]
    qg = jnp.concatenate([q[:, r * HEAD_DIM:(r + 1) * HEAD_DIM] for r in range(Q_PER_KV)], axis=0)

    lane = lax.broadcasted_iota(jnp.int32, (rows, 256), 1)
    cbr = pltpu.roll(cb_ref[0], lax.rem(4 * i + 244, 256), 1)
    s = _dot_nt(qg, kc_ref[0, 0]) * C1 + jnp.where(lane >= 4 * i + 4, NEG, cbr)
    mx = jnp.max(s, axis=1, keepdims=True)
    mx = jnp.where(mx > 0.5 * NEG, mx, 0.0)
    e = jnp.exp2(s - mx)
    p = e * (1.0 / jnp.maximum(jnp.sum(e, axis=1, keepdims=True), 1e-30))
    o_c = _dot(p.astype(BF16), vc_ref[0, 0])

    imp = p[0:TQ] + p[TQ:2 * TQ] + p[2 * TQ:3 * TQ] + p[3 * TQ:4 * TQ]
    st = st_ref[...]
    h1, h2, h3 = _split3(imp)
    ps_t = _dot_nt(st, h1) + _dot_nt(st, h2) + _dot_nt(st, h3)
    jj = lax.broadcasted_iota(jnp.int32, (64, TQ), 0)
    forced = (jj == 0) | (jj == i) | (jj == i - 1)
    sc = jnp.where(jj > i, NEG_BIG, jnp.where(forced, POS_BIG, ps_t))
    sel_t = _rank_select(sc, 64, jj, i)
    eye = jnp.where(lax.broadcasted_iota(jnp.int32, (TQ, TQ), 0) == lax.broadcasted_iota(jnp.int32, (TQ, TQ), 1),
                    1.0, 0.0).astype(BF16)
    sel_t = jnp.concatenate([sel_t, jnp.zeros((64, TQ), F32)], axis=0).astype(BF16)
    sel = _dot_nt(eye, sel_t)

    jl = lax.broadcasted_iota(jnp.int32, (TQ, 128), 1)

    def mask_lanes(keep):
        m = jnp.where(jl <= 64, 1.0 - keep, 0.0) * NEG
        return jnp.concatenate([m] * Q_PER_KV, axis=0).astype(BF16)

    q_near = jnp.concatenate([qg, mask_lanes(sel)], axis=1)
    q_far = jnp.concatenate([qg, mask_lanes(jnp.where(jl <= i - 4, sel, 0.0))], axis=1)

    def keys(ref, base, n, id_ref):
        return jnp.concatenate([ref[0, pl.ds(base, n), :], id_ref[pl.ds(base, n), :]], axis=1)

    def vals(ref, base, n):
        return jnp.concatenate([ref[0, pl.ds(base, n), :], one_ref[0:n, :]], axis=1)

    btn = btn_ref[0]
    base_n = pl.multiple_of(KV_PAD + (i - 3) * 64, 64)
    base_m = pl.multiple_of(KV_PAD + (i - 11) * 64, 64)

    s = _dot_nt(q_near, jnp.concatenate([keys(ks_ref, base_n, 256, et_ref), keys(kw_ref, base_m, 768, pad_ref)],
                                        axis=0)) * C1
    m_s, a_s = _first_aug(s[:, 0:256] + btn, vals(vs_ref, base_n, 256))
    _, a_w = _first_aug(s[:, 256:1024] + jnp.concatenate([pm_ref[...], btn], axis=1), vals(vw_ref, base_m, 768))
    o_w = _normalise_aug(a_w)


    def far_body(c, carry):
        base = pl.multiple_of(KV_PAD + c * 512, 512)
        sf = _dot_nt(q_far, keys(ks_ref, base, 512, et_ref)) * C1
        return _online_aug(sf, vals(vs_ref, base, 512), *carry)

    n_far = jnp.where(i >= 4, (i - 4) // 8 + 1, 0)
    m_s, a_s = lax.fori_loop(0, n_far, far_body, (m_s, a_s))
    o_s = _normalise_aug(a_s)

    gt = jax.nn.sigmoid(gt_ref[0, 0])

    def gcol(br):
        return jnp.concatenate([gt[:, 3 * r + br:3 * r + br + 1] for r in range(Q_PER_KV)], axis=0)

    out = gcol(0) * o_c + gcol(1) * o_s + gcol(2) * o_w
    for r in range(Q_PER_KV):
        o_ref[0, :, r * HEAD_DIM:(r + 1) * HEAD_DIM] = out[r * TQ:(r + 1) * TQ].astype(o_ref.dtype)


def _attn_prompt(q, gt, kc, vc, ks, kw, btn, cb, pm, st):
    b, t, _ = q.shape
    lp = ks.shape[1]
    nc = kc.shape[2]
    assert nc == 256 and t // SEL_BLOCK == 64
    kspec = pl.BlockSpec((1, lp, HEAD_DIM), lambda bb, g, i: (bb, 0, g))
    vspec = pl.BlockSpec((1, lp, HEAD_DIM), lambda bb, g, i: (bb, 0, N_KV + g))
    row = np.arange(lp)[:, None]
    lane = np.arange(128)[None, :]
    et = jnp.asarray(np.where(row >= KV_PAD, lane == (row - KV_PAD) // SEL_BLOCK, lane == 64).astype(np.float32), BF16)
    pad = jnp.asarray(((row < KV_PAD) & (lane == 64)).astype(np.float32), BF16)
    ones = jnp.ones((768, 128), BF16)
    return pl.pallas_call(
        _attn_prompt_kernel,
        name="attn_prompt",
        grid=(b, N_KV, t // TQ),
        in_specs=[
            pl.BlockSpec((1, TQ, Q_PER_KV * HEAD_DIM), lambda bb, g, i: (bb, i, g)),
            pl.BlockSpec((1, 1, TQ, 12), lambda bb, g, i: (bb, g, i, 0)),
            pl.BlockSpec((1, 1, nc, HEAD_DIM), lambda bb, g, i: (bb, g, 0, 0)),
            pl.BlockSpec((1, 1, nc, HEAD_DIM), lambda bb, g, i: (bb, g, 0, 0)),
            kspec, vspec, kspec, vspec,
            pl.BlockSpec((1, Q_PER_KV * TQ, 256), lambda bb, g, i: (g, 0, 0)),
            pl.BlockSpec((1, Q_PER_KV * TQ, 256), lambda bb, g, i: (g, 0, 0)),
            pl.BlockSpec((Q_PER_KV * TQ, 512), lambda bb, g, i: (0, 0)),
            pl.BlockSpec((64, 256), lambda bb, g, i: (0, 0)),
            pl.BlockSpec((lp, 128), lambda bb, g, i: (0, 0)),
            pl.BlockSpec((lp, 128), lambda bb, g, i: (0, 0)),
            pl.BlockSpec((768, 128), lambda bb, g, i: (0, 0)),
        ],
        out_specs=pl.BlockSpec((1, TQ, Q_PER_KV * HEAD_DIM), lambda bb, g, i: (bb, i, g)),
        out_shape=jax.ShapeDtypeStruct((b, t, N_HEADS * HEAD_DIM), BF16),
        compiler_params=_cparams("parallel", "parallel", "arbitrary"),
    )(q, gt, kc, vc, ks, ks, kw, kw, btn, cb, pm, st, et, pad, ones)


S_DEC = 8
S_ROWS = N_HEADS * S_DEC
PAGES_PER_STEP = 8
N_PAGES = PAST_LEN // PAGE
N_STEPS = N_PAGES // PAGES_PER_STEP
NC_S = PAST_LEN // CMP_STRIDE
CUR_S = PAST_LEN // SEL_BLOCK
NJ_S = 264
GW = N_KV * HEAD_DIM


def _diag_blocks(x):
    return jnp.concatenate([x[g * 32:(g + 1) * 32, g * HEAD_DIM:(g + 1) * HEAD_DIM] for g in range(N_KV)], axis=0)


def _attn_sample_kernel(pt_ref, q_ref, gt_ref, kc_ref, vc_ref, *refs):
    del pt_ref
    pages = refs[:PAGES_PER_STEP]
    (snew_ref, wbuf_ref, wnew_ref, tc_ref, tl_ref, tn_ref, tw_ref, st_ref, e16_ref,
     o_ref, qb_ref, sc_ref, sel_ref, m_ref, l_ref, acc_ref, oc_ref) = refs[PAGES_PER_STEP:]
    step = pl.program_id(1)

    @pl.when(step == 0)
    def _():
        q = q_ref[0].astype(F32)
        zero = jnp.zeros((32, HEAD_DIM), F32)
        qgs, qb_rows = [], []
        for g in range(N_KV):
            qg = jnp.concatenate([q[:, (4 * g + r) * HEAD_DIM:(4 * g + r + 1) * HEAD_DIM]
                                  for r in range(Q_PER_KV)], axis=0)
            qgs.append(qg.astype(BF16))
            qb_rows.append(jnp.concatenate([qg if gg == g else zero for gg in range(N_KV)], axis=1))
        qb = jnp.concatenate(qb_rows, axis=0).astype(BF16)
        qb_ref[...] = qb

        s = jnp.concatenate([_dot_nt(qgs[g], kc_ref[0, g]) for g in range(N_KV)], axis=0) * C1 + tc_ref[...]
        mx = jnp.max(s, axis=1, keepdims=True)
        mx = jnp.where(mx > 0.5 * NEG, mx, 0.0)
        e = jnp.exp2(s - mx)
        p = e * (1.0 / jnp.maximum(jnp.sum(e, axis=1, keepdims=True), 1e-30))
        pb = p.astype(BF16)
        oc_ref[...] = jnp.concatenate([_dot(pb[g * 32:(g + 1) * 32], vc_ref[0, g]) for g in range(N_KV)], axis=0)

        reps = []
        for g in range(N_KV):
            pg = p[g * 32:(g + 1) * 32]
            ig = pg[0:8] + pg[8:16] + pg[16:24] + pg[24:32]
            reps += [ig] * Q_PER_KV
        imp = jnp.concatenate(reps, axis=0)
        st = st_ref[...]
        h1, h2, h3 = _split3(imp)
        ps_t = _dot_nt(st, h1) + _dot_nt(st, h2) + _dot_nt(st, h3)
        jj = lax.broadcasted_iota(jnp.int32, (NJ_S, S_ROWS), 0)
        forced = (jj == 0) | (jj == CUR_S) | (jj == CUR_S - 1)
        sc = jnp.where(jj > CUR_S, NEG_BIG, jnp.where(forced, POS_BIG, ps_t))
        sc_ref[...] = sc
        sel_ref[...] = jnp.zeros((NJ_S, S_ROWS), F32)
        jf = jj.astype(F32)

        def take_best(_, carry):
            v = sc_ref[...]
            best = jnp.max(v, axis=0, keepdims=True)
            first = jnp.min(jnp.where(v == best, jf, float(NJ_S)), axis=0, keepdims=True)
            hit = jf == first
            sel_ref[...] = jnp.where(hit, 1.0, sel_ref[...])
            sc_ref[...] = jnp.where(hit, 2.0 * NEG_BIG, v)
            return carry

        lax.fori_loop(0, N_SEL, take_best, 0)
        sel_ref[...] = jnp.where(jj <= CUR_S, sel_ref[...], 0.0)

        new = jnp.concatenate([snew_ref[0], jnp.zeros((128 - S_DEC, 2 * GW), F32)], axis=0).astype(BF16)
        sn = _dot_nt(qb, new[:, :GW]) * C1 + tn_ref[...]
        m0, l0, a0 = _first(sn, new[:, GW:])
        m_ref[...] = m0
        l_ref[...] = l0
        acc_ref[...] = a0

    def gather_rows(refs, n_tok, first):
        return jnp.concatenate(
            [jnp.concatenate([r[0, pl.ds(first + g, n_tok, stride=2 * N_KV), :] for r in refs], axis=0)
             for g in range(N_KV)], axis=1)

    kb = gather_rows(pages, PAGE, 0).astype(BF16)
    vb = gather_rows(pages, PAGE, N_KV).astype(BF16)
    qb = qb_ref[...]
    s = _dot_nt(qb, kb) * C1 + jnp.where(step == N_STEPS - 1, 1.0, 0.0) * tl_ref[...]
    blocks = PAGES_PER_STEP * PAGE // SEL_BLOCK
    sel_step = sel_ref[pl.ds(pl.multiple_of(step * blocks, blocks), blocks), :]
    eye = jnp.where(lax.broadcasted_iota(jnp.int32, (S_ROWS, S_ROWS), 0)
                    == lax.broadcasted_iota(jnp.int32, (S_ROWS, S_ROWS), 1), 1.0, 0.0).astype(BF16)
    sel_r = _dot_nt(eye, sel_step.astype(BF16)).astype(BF16)
    s = jnp.where(_dot(sel_r, e16_ref[...]) > 0.5, s, NEG)
    m1, l1, a1 = _online(s, vb, m_ref[...], l_ref[...], acc_ref[...])
    m_ref[...] = m1
    l_ref[...] = l1
    acc_ref[...] = a1

    @pl.when(step == N_STEPS - 1)
    def _():
        o_s = _diag_blocks(a1 * (1.0 / l1))
        wn = jnp.concatenate([wnew_ref[0], jnp.zeros((128 - S_DEC, 2 * GW), F32)], axis=0)
        wk = jnp.concatenate([gather_rows([wbuf_ref], WINDOW, 0), wn[:, :GW]], axis=0).astype(BF16)
        wv = jnp.concatenate([gather_rows([wbuf_ref], WINDOW, N_KV), wn[:, GW:]], axis=0).astype(BF16)
        sw = _dot_nt(qb, wk) * C1 + tw_ref[...]
        _, lw, aw = _first(sw, wv)
        o_w = _diag_blocks(aw * (1.0 / lw))
        gt = jax.nn.sigmoid(gt_ref[0])
        out = gt[:, 0:1] * oc_ref[...] + gt[:, 1:2] * o_s + gt[:, 2:3] * o_w
        for h in range(N_HEADS):
            o_ref[0, :, h * HEAD_DIM:(h + 1) * HEAD_DIM] = out[h * S_DEC:(h + 1) * S_DEC]


def _attn_sample(q, gt, kc, vc, cache, page_table, page_base, snew, wbuf, wnew, layer, tabs):
    b = q.shape[0]
    tc, tl, tn, tw, st, e16 = tabs
    rows_per_tok = 2 * N_KV

    def page_spec(j):
        return pl.BlockSpec((1, PAGE * rows_per_tok, HEAD_DIM),
                            lambda bb, s, pt: (page_base + pt[bb, s * PAGES_PER_STEP + j], 0, 0))

    def full(shape):
        return pl.BlockSpec(shape, lambda bb, s, pt: tuple(0 for _ in shape))

    grid_spec = pltpu.PrefetchScalarGridSpec(
        num_scalar_prefetch=1,
        grid=(b, N_STEPS),
        in_specs=[
            pl.BlockSpec((1, S_DEC, N_HEADS * HEAD_DIM), lambda bb, s, pt: (bb, 0, 0)),
            pl.BlockSpec((1, S_ROWS, 3), lambda bb, s, pt: (bb, 0, 0)),
            pl.BlockSpec((1, N_KV, NC_S, HEAD_DIM), lambda bb, s, pt: (bb, 0, 0, 0)),
            pl.BlockSpec((1, N_KV, NC_S, HEAD_DIM), lambda bb, s, pt: (bb, 0, 0, 0)),
        ] + [page_spec(j) for j in range(PAGES_PER_STEP)] + [
            pl.BlockSpec((1, S_DEC, 2 * GW), lambda bb, s, pt: (bb, 0, 0)),
            pl.BlockSpec((None, 1, WINDOW * rows_per_tok, HEAD_DIM), lambda bb, s, pt: (layer, bb, 0, 0)),
            pl.BlockSpec((1, S_DEC, 2 * GW), lambda bb, s, pt: (bb, 0, 1)),
            full(tc.shape), full(tl.shape), full(tn.shape), full(tw.shape), full(st.shape), full(e16.shape),
        ],
        out_specs=pl.BlockSpec((1, S_DEC, N_HEADS * HEAD_DIM), lambda bb, s, pt: (bb, 0, 0)),
        scratch_shapes=[pltpu.VMEM((S_ROWS, GW), BF16), pltpu.VMEM((NJ_S, S_ROWS), F32),
                        pltpu.VMEM((NJ_S, S_ROWS), F32), pltpu.VMEM((S_ROWS, 1), F32), pltpu.VMEM((S_ROWS, 1), F32),
                        pltpu.VMEM((S_ROWS, GW), F32), pltpu.VMEM((S_ROWS, HEAD_DIM), F32)],
    )
    return pl.pallas_call(
        _attn_sample_kernel,
        name="attn_sample",
        grid_spec=grid_spec,
        out_shape=jax.ShapeDtypeStruct((b, S_DEC, N_HEADS * HEAD_DIM), F32),
        compiler_params=_cparams("parallel", "arbitrary"),
    )(page_table, q, gt, kc, vc, *([cache] * PAGES_PER_STEP), snew, wbuf, wnew, tc, tl, tn, tw, st, e16)


def _sample_tables(rel_bias):
    s = np.arange(S_DEC)[:, None]
    qpos = PAST_LEN + s

    def rows(tab):
        return tab.reshape(S_ROWS, tab.shape[-1])

    c = np.arange(NC_S)[None, :]
    d = qpos - (CMP_STRIDE * c + 2 * CMP_STRIDE - 1)
    tc = rows(_bias_table(rel_bias, _idx_np(d, d >= 0)))
    k = np.arange(PAGES_PER_STEP * PAGE)[None, :]
    d = qpos - (PAST_LEN - PAGES_PER_STEP * PAGE + k)
    tl = rows(_bias_table(rel_bias, _idx_np(d, d >= 0)))
    k = np.arange(128)[None, :]
    d = s - k
    tn = rows(_bias_table(rel_bias, _idx_np(d, (d >= 0) & (k < S_DEC))))
    k = np.arange(WINDOW + 128)[None, :]
    d = np.where(k < WINDOW, qpos - (PAST_LEN - WINDOW + k), s - (k - WINDOW))
    tw = rows(_bias_table(rel_bias, _idx_np(d, (d >= 0) & (d < WINDOW) & (k < WINDOW + S_DEC))))
    st = jnp.asarray(_sel_matrix_np(NJ_S, NC_S, NC_S - 1), BF16)
    blocks = PAGES_PER_STEP * PAGE // SEL_BLOCK
    e16 = jnp.asarray((np.arange(blocks)[:, None] == np.arange(PAGES_PER_STEP * PAGE)[None, :] // SEL_BLOCK)
                      .astype(np.float32), BF16)
    return tc, tl, tn, tw, st, e16


def _projections(hf, w_main, w_gn, w_gm, layer):
    za = _mm(hf, w_main, layer, 0, 4096, F32, w_t=True)
    q = _mm(hf, w_main, layer, 4096, 2048, BF16, w_t=True)
    cmp = _mm(hf, w_main, layer, 6144, 1024, F32, w_t=True)
    sw = _mm(hf, w_main, layer, 7168, 2048, F32, w_t=True)
    gn = _mm(hf, w_gn, layer, 0, 48, F32, w_t=True)
    gm = _mm(hf, w_gm, layer, 0, 3 * D_MODEL, F32, w_t=True)
    return za, q, cmp, sw, gn, gm


def kernel(x_prompt, x_sample, cache_cmp_kv, cache_slc_kv, state_win_kv, state_conv, state_pool, page_table,
           c_prompt, c_sample, rel_bias, norm_g, ada_w, ada_b, w_in, conv_w, pool_w, pool_scale,
           cmp_pe, cmp_w1, cmp_w2, w_branch, w_out, mlp_w1, mlp_w2):
    bp, t, d = x_prompt.shape
    bs, s_len, _ = x_sample.shape
    n_phys = cache_cmp_kv.shape[1]
    assert (t, d, s_len, bs) == (4096, D_MODEL, S_DEC, 8)

    n_main = 9216
    w_main = jnp.swapaxes(w_in, 1, 2)
    w_gn = w_main[:, n_main:n_main + 48]
    w_gm = w_main[:, n_main + 48:]
    ada_wb = ada_w
    w_branch_b = w_branch.astype(BF16)
    w_out_b = w_out
    w1_b = mlp_w1
    w2_b = mlp_w2
    pool_wb = pool_w.astype(BF16)
    w1r = cmp_w1.reshape(DEPTH, 2, 2 * CMP_STRIDE, HEAD_DIM, HEAD_DIM)
    half = CMP_STRIDE * HEAD_DIM
    w1ab = jnp.concatenate([w1r[:, :, :CMP_STRIDE].reshape(DEPTH, 2, half, HEAD_DIM),
                            w1r[:, :, CMP_STRIDE:].reshape(DEPTH, 2, half, HEAD_DIM)], axis=-1).astype(BF16)
    pe = cmp_pe.reshape(DEPTH, 2, 1, 2 * half)
    w2c = cmp_w2.astype(BF16)

    c16 = jnp.concatenate([c_prompt, c_sample, jnp.zeros((16 - bp - bs, d), F32)], axis=0)
    mod = _ada(c16, ada_wb, ada_b)
    mod_p = mod[:, :bp].reshape(DEPTH, bp, 6, d)
    mod_s = mod[:, bp:bp + bs].reshape(DEPTH, bs, 6, d)

    btn, cb = _prompt_tables(rel_bias)
    pm = jnp.asarray(_window_mid_mask_np())
    st_p = jnp.asarray(_sel_matrix_np(t // SEL_BLOCK, t // CMP_STRIDE, t // CMP_STRIDE - 1), BF16)
    tabs_s = _sample_tables(rel_bias)

    rpt = 2 * N_KV
    cache_c = cache_cmp_kv.reshape(DEPTH * n_phys, PAGE * rpt, HEAD_DIM)
    cache_s = cache_slc_kv.reshape(DEPTH * n_phys, PAGE * rpt, HEAD_DIM)
    wbuf = state_win_kv.reshape(DEPTH, bs, WINDOW * rpt, HEAD_DIM)
    conv0 = jnp.zeros((bp, 2, D_CONV), F32)
    pool0 = jnp.zeros((bp, 15, D_POOL), F32)
    dummy_pt = jnp.zeros((1, 1), jnp.int32)

    xp, xs = x_prompt, x_sample
    _, hp = _norm(xp, None, norm_g, mod_p, None, (0, 0, 0, 1))
    _, hs = _norm(xs, None, norm_g, mod_s, None, (0, 0, 0, 1))
    outs = [[] for _ in range(10)]
    mp, ms = bp * t, bs * s_len
    for l in range(DEPTH):
        za, q, cmp, sw, gn, gm = _projections(hp.reshape(mp, d), w_main, w_gn, w_gm, l)
        ya, yb, cst_p, pst_p = _convpool(za.reshape(bp, t, 4096), conv0, pool0, conv_w, pool_wb, pool_scale, l, 0)
        n16 = t // CMP_STRIDE
        kc, vc = _compress([cmp.reshape(bp, n16, CMP_STRIDE * 2 * GW)],
                           [pl.BlockSpec((1, 64, CMP_STRIDE * 2 * GW), lambda b, s, p: (b, s, 0))],
                           dummy_pt, bp, n16 // 64, 64, w1ab, pe, w2c, l)
        swp = sw.reshape(bp, t, 4 * GW)
        ks = jnp.pad(swp[:, :, :2 * GW].astype(BF16), ((0, 0), (KV_PAD, 0), (0, 0)))
        kw = jnp.pad(swp[:, :, 2 * GW:].astype(BF16), ((0, 0), (KV_PAD, 0), (0, 0)))
        gt = gn.reshape(bp, t, N_KV, 12).transpose(0, 2, 1, 3)
        yc = _attn_prompt(q.reshape(bp, t, 2048), gt, kc, vc, ks, kw, btn, cb, pm, st_p)
        merged = _merge(ya.reshape(mp, D_CONV), yb.reshape(mp, D_POOL), yc.reshape(mp, 2048), gm, w_branch_b, l)
        mo = _mm(merged, w_out_b, l, 0, d, F32)
        xp, h2 = _norm(xp, mo.reshape(bp, t, d), norm_g, mod_p, (l, 1, 2), (l, 2, 3, 4))
        u = _mm(h2.reshape(mp, d), w1_b, l, 0, D_FF, BF16, act="relu2")
        f = _mm(u, w2_b, l, 0, d, F32)
        xp, hp = _norm(xp, f.reshape(bp, t, d), norm_g, mod_p, (l, 3, 5), (l + 1, 0, 0, 1) if l + 1 < DEPTH else None)
        sw5 = swp.reshape(bp, t, 2, 2, N_KV, HEAD_DIM)
        outs[0].append(cmp.reshape(bp, t, 2, N_KV, HEAD_DIM))
        outs[2].append(sw5[:, :, 0])
        outs[4].append(sw5[:, t - WINDOW:, 1])
        outs[6].append(cst_p)
        outs[8].append(pst_p)

        za, q, cmp, sw, gn, gm = _projections(hs.reshape(ms, d), w_main, w_gn, w_gm, l)
        ya, yb, cst_s, pst_s = _convpool(za.reshape(bs, s_len, 4096), state_conv[l], state_pool[l],
                                         conv_w, pool_wb, pool_scale, l, PAST_LEN)
        sws = sw.reshape(bs, s_len, 4 * GW)
        cpp = PAGE // CMP_STRIDE
        page_specs = [pl.BlockSpec((1, PAGE * rpt, HEAD_DIM),
                                   lambda b, s, p, j=j, l=l: (l * n_phys + p[b, s * PAGES_PER_STEP + j], 0, 0))
                      for j in range(PAGES_PER_STEP)]
        kc, vc = _compress([cache_c] * PAGES_PER_STEP, page_specs, page_table, bs, N_STEPS, cpp, w1ab, pe, w2c, l,
                           paged=True)
        gts = gn.reshape(bs, s_len, N_KV, Q_PER_KV, 3).transpose(0, 2, 3, 1, 4).reshape(bs, S_ROWS, 3)
        yc = _attn_sample(q.reshape(bs, s_len, 2048), gts, kc, vc, cache_s, page_table, l * n_phys,
                          sws, wbuf, sws, l, tabs_s)
        merged = _merge(ya.reshape(ms, D_CONV), yb.reshape(ms, D_POOL), yc.reshape(ms, 2048), gm, w_branch_b, l)
        mo = _mm(merged, w_out_b, l, 0, d, F32)
        xs, h2 = _norm(xs, mo.reshape(bs, s_len, d), norm_g, mod_s, (l, 1, 2), (l, 2, 3, 4))
        u = _mm(h2.reshape(ms, d), w1_b, l, 0, D_FF, BF16, act="relu2")
        f = _mm(u, w2_b, l, 0, d, F32)
        xs, hs = _norm(xs, f.reshape(bs, s_len, d), norm_g, mod_s, (l, 3, 5), (l + 1, 0, 0, 1) if l + 1 < DEPTH else None)
        sw5 = sws.reshape(bs, s_len, 2, 2, N_KV, HEAD_DIM)
        outs[1].append(cmp.reshape(bs, s_len, 2, N_KV, HEAD_DIM))
        outs[3].append(sw5[:, :, 0])
        outs[5].append(jnp.concatenate([state_win_kv[l][:, s_len:], sw5[:, :, 1]], axis=1))
        outs[7].append(cst_s)
        outs[9].append(pst_s)

    return (xp, xs) + tuple(jnp.stack(o) for o in outs)
```

```python
import functools
import math

import numpy as np
import jax
import jax.numpy as jnp
from jax import lax
from jax.experimental import pallas as pl
from jax.experimental.pallas import tpu as pltpu

F32 = jnp.float32
BF16 = jnp.bfloat16

D_MODEL = 2048
DEPTH = 4
PAST_LEN = 16384
PAGE = 128
D_CONV = 1024
D_POOL = 1024
POOL_WINDOWS = (2, 4, 8, 16)
POOL_GROUP = 256
HEAD_DIM = 128
N_KV = 4
Q_PER_KV = 4
N_HEADS = 16
CMP_STRIDE = 16
SEL_BLOCK = 64
N_SEL = 16
WINDOW = 512
N_BUCKETS = 32
MAX_DISTANCE = 128
D_FF = 8192
N_MAIN = 9216
RMS_EPS = 1e-6
NEG_BIG = -1e9
POS_BIG = 1e9
SCALE = HEAD_DIM ** -0.5
LOG2E = math.log2(math.e)
C1 = SCALE * LOG2E

NEG = -1e30
KV_PAD = 1024
VMEM_LIMIT = 56 * 1024 * 1024
TQ = 64


def _cparams(*sem):
    return pltpu.CompilerParams(dimension_semantics=sem, vmem_limit_bytes=VMEM_LIMIT)


def _dot(a, b):
    return jnp.dot(a, b, preferred_element_type=F32)


def _dot_nt(a, b):
    return lax.dot_general(a, b, (((1,), (1,)), ((), ())), preferred_element_type=F32)


def _split3(x):
    h1 = x.astype(BF16)
    r1 = x - h1.astype(F32)
    h2 = r1.astype(BF16)
    h3 = (r1 - h2.astype(F32)).astype(BF16)
    return h1, h2, h3


def _mm_kernel(a_ref, w_ref, o_ref, *scratch, nk, act, stage_w, w_t):
    if len(w_ref.shape) == 3:
        w_ref = w_ref.at[0]
    if stage_w:
        wb_ref = scratch[-1]

        @pl.when(pl.program_id(1) == 0)
        def _():
            wb_ref[...] = w_ref[...].astype(BF16)

        w = wb_ref[...]
    else:
        w = w_ref[...].astype(BF16)
    part = (_dot_nt if w_t else _dot)(a_ref[...].astype(BF16), w)

    def finish(acc):
        if act == "relu2":
            r = jnp.maximum(acc, 0.0)
            acc = r * r
        o_ref[...] = acc.astype(o_ref.dtype)

    if nk == 1:
        finish(part)
    else:
        acc_ref = scratch[0]
        k = pl.program_id(2)

        @pl.when(k == 0)
        def _():
            acc_ref[...] = part

        @pl.when(k > 0)
        def _():
            acc_ref[...] += part

        @pl.when(k == nk - 1)
        def _():
            finish(acc_ref[...])


def _mm(a, w, layer, col0, n, out_dtype, act=None, tn=512, w_t=False):
    m, k = a.shape
    tm = min(m, 1024)
    tk = min(k, 2048)
    tn = min(tn, n)
    nk = k // tk
    assert m % tm == 0 and k % tk == 0 and n % tn == 0 and (col0 % tn == 0 or (w_t and col0 % 8 == 0))
    cb0 = col0 // tn
    stage_w = nk == 1 and m // tm > 1 and w.dtype != BF16
    wblock = (tn, tk) if w_t else (tk, tn)
    scratch = ([pltpu.VMEM((tm, tn), F32)] if nk > 1 else []) + ([pltpu.VMEM(wblock, BF16)] if stage_w else [])
    if w_t and col0 % tn:
        wspec = pl.BlockSpec((pl.Element(1), pl.Element(tn), pl.Element(tk)),
                             lambda j, i, kk: (layer, pl.multiple_of(col0 + j * tn, 8), kk * tk))
    elif w_t:
        wspec = pl.BlockSpec((None, tn, tk), lambda j, i, kk: (layer, cb0 + j, kk))
    else:
        wspec = pl.BlockSpec((None, tk, tn), lambda j, i, kk: (layer, kk, cb0 + j))
    return pl.pallas_call(
        functools.partial(_mm_kernel, nk=nk, act=act, stage_w=stage_w, w_t=w_t),
        name="mm",
        grid=(n // tn, m // tm, nk),
        in_specs=[pl.BlockSpec((tm, tk), lambda j, i, kk: (i, kk)), wspec],
        out_specs=pl.BlockSpec((tm, tn), lambda j, i, kk: (i, j)),
        out_shape=jax.ShapeDtypeStruct((m, n), out_dtype),
        scratch_shapes=scratch,
        compiler_params=_cparams("parallel", "arbitrary", "arbitrary"),
    )(a, w)


def _merge_kernel(ya_ref, yb_ref, yc_ref, g0_ref, g1_ref, g2_ref, wa_ref, wb_ref, wc_ref, o_ref):
    acc = jax.nn.sigmoid(g0_ref[...]) * _dot(ya_ref[...].astype(BF16), wa_ref[...])
    acc += jax.nn.sigmoid(g1_ref[...]) * _dot(yb_ref[...].astype(BF16), wb_ref[...])
    acc += jax.nn.sigmoid(g2_ref[...]) * _dot(yc_ref[...].astype(BF16), wc_ref[...])
    o_ref[...] = acc.astype(o_ref.dtype)


def _merge(ya, yb, yc, gm, w_branch, layer):
    m = ya.shape[0]
    tm = min(m, 1024)
    tn = 512
    nb = D_MODEL // tn
    return pl.pallas_call(
        _merge_kernel,
        name="merge",
        grid=(nb, m // tm),
        in_specs=[
            pl.BlockSpec((tm, D_CONV), lambda j, i: (i, 0)),
            pl.BlockSpec((tm, D_POOL), lambda j, i: (i, 0)),
            pl.BlockSpec((tm, D_MODEL), lambda j, i: (i, 0)),
            pl.BlockSpec((tm, tn), lambda j, i: (i, j)),
            pl.BlockSpec((tm, tn), lambda j, i: (i, nb + j)),
            pl.BlockSpec((tm, tn), lambda j, i: (i, 2 * nb + j)),
            pl.BlockSpec((None, D_CONV, tn), lambda j, i: (layer, 0, j)),
            pl.BlockSpec((None, D_POOL, tn), lambda j, i: (layer, 1, j)),
            pl.BlockSpec((None, D_MODEL, tn), lambda j, i: (layer, 1, j)),
        ],
        out_specs=pl.BlockSpec((tm, tn), lambda j, i: (i, j)),
        out_shape=jax.ShapeDtypeStruct((m, D_MODEL), BF16),
        compiler_params=_cparams("parallel", "parallel"),
    )(ya, yb, yc, gm, gm, gm, w_branch, w_branch, w_branch)


def _ada_kernel(c_ref, w_ref, b_ref, o_ref):
    c = c_ref[...]
    a = (c * jax.nn.sigmoid(c)).astype(BF16)
    o_ref[...] = _dot(a, w_ref[...].astype(BF16)) + b_ref[...]


def _ada(c16, ada_w, ada_b):
    tn = 1024
    n = 6 * D_MODEL
    return pl.pallas_call(
        _ada_kernel,
        name="ada",
        grid=(DEPTH, n // tn),
        in_specs=[
            pl.BlockSpec((16, D_MODEL), lambda l, j: (0, 0)),
            pl.BlockSpec((None, D_MODEL, tn), lambda l, j: (l, 0, j)),
            pl.BlockSpec((None, 1, tn), lambda l, j: (l, 0, j)),
        ],
        out_specs=pl.BlockSpec((None, 16, tn), lambda l, j: (l, 0, j)),
        out_shape=jax.ShapeDtypeStruct((DEPTH, 16, n), F32),
        compiler_params=_cparams("parallel", "parallel"),
    )(c16, ada_w, ada_b.reshape(DEPTH, 1, n))


def _rms(x, g):
    return x * lax.rsqrt(jnp.mean(x * x, axis=-1, keepdims=True) + RMS_EPS) * g


def _norm_kernel(*refs, res, hh):
    it = iter(refs)
    x = next(it)[0]
    if res is not None:
        m_ref, ng_ref, mod_ref = next(it), next(it), next(it)
    if hh is not None:
        ngh_ref, modh_ref = next(it), next(it)
    if res is not None:
        _, ng_row, gate_row = res
        x = x + mod_ref[0, gate_row:gate_row + 1, :] * _rms(m_ref[0], ng_ref[ng_row:ng_row + 1, :])
        next(it)[0] = x
    if hh is not None:
        _, ng_row, shift_row, scale_row = hh
        h = (_rms(x, ngh_ref[ng_row:ng_row + 1, :]) * (1.0 + modh_ref[0, scale_row:scale_row + 1, :])
             + modh_ref[0, shift_row:shift_row + 1, :])
        next(it)[0] = h.astype(BF16)


def _norm(x, m, norm_g, mod, res, hh):
    b, t, d = x.shape
    tr = min(t, 512)
    xspec = pl.BlockSpec((1, tr, d), lambda bb, i: (bb, i, 0))

    def params(layer):
        return ([norm_g, mod],
                [pl.BlockSpec((None, 4, d), lambda bb, i: (layer, 0, 0)),
                 pl.BlockSpec((None, 1, 6, d), lambda bb, i: (layer, bb, 0, 0))])

    ins, specs, outs, ospecs = [x], [xspec], [], []
    if res is not None:
        a, s = params(res[0])
        ins += [m] + a
        specs += [xspec] + s
        outs.append(jax.ShapeDtypeStruct((b, t, d), F32))
        ospecs.append(xspec)
    if hh is not None:
        a, s = params(hh[0])
        ins += a
        specs += s
        outs.append(jax.ShapeDtypeStruct((b, t, d), BF16))
        ospecs.append(xspec)
    out = pl.pallas_call(
        functools.partial(_norm_kernel, res=res, hh=hh),
        name="norm",
        grid=(b, t // tr),
        in_specs=specs,
        out_specs=ospecs,
        out_shape=outs,
        compiler_params=_cparams("parallel", "parallel"),
    )(*ins)
    out = list(out)
    x_new = out.pop(0) if res is not None else None
    h = out.pop(0) if hh is not None else None
    return x_new, h


def _convpool_kernel(ab_ref, ac_ref, ax_ref, p_ref, cpre_ref, ppre_ref, cw_ref, pw_ref, ps_ref,
                     ya_ref, yb_ref, cst_ref, pst_ref, u_ext, p_ext, *, tr, nt, pos0):
    t = pl.program_id(1)

    @pl.when(t == 0)
    def _():
        u_ext[0:8, :] = jnp.zeros((8, D_CONV), F32)
        u_ext[6:8, :] = cpre_ref[0]
        p_ext[0:16, :] = jnp.zeros((16, D_POOL), F32)
        p_ext[1:16, :] = ppre_ref[0]

    u = ac_ref[0] * ax_ref[0]
    u_ext[8:8 + tr, :] = u
    v = cw_ref[0:1, :] * u_ext[6:6 + tr, :] + cw_ref[1:2, :] * u_ext[7:7 + tr, :] + cw_ref[2:3, :] * u
    ya_ref[0] = (ab_ref[0] * v).astype(ya_ref.dtype)

    p = p_ref[0]
    p_ext[16:16 + tr, :] = p
    pos = pos0 + t * tr + lax.broadcasted_iota(jnp.int32, (tr, 1), 0)
    for g, w in enumerate(POOL_WINDOWS):
        lo = g * POOL_GROUP
        s = p[:, lo:lo + POOL_GROUP]
        for k in range(1, w):
            s = s + p_ext[16 - k:16 - k + tr, lo:lo + POOL_GROUP]
        cnt = jnp.minimum(pos + 1, w).astype(F32)
        dlt = s / cnt - p[:, lo:lo + POOL_GROUP]
        y = _dot(dlt.astype(BF16), pw_ref[g])
        yb_ref[0, :, lo:lo + POOL_GROUP] = (y * ps_ref[:, lo:lo + POOL_GROUP]).astype(yb_ref.dtype)

    @pl.when(t == nt - 1)
    def _():
        cst_ref[0] = u_ext[tr + 6:tr + 8, :]
        pst_ref[0] = p_ext[tr + 1:tr + 16, :]

    if nt > 1:
        u_ext[0:8, :] = u_ext[tr:tr + 8, :]
        p_ext[0:16, :] = p_ext[tr:tr + 16, :]


def _convpool(za, conv_pre, pool_pre, conv_w, pool_w, pool_scale, layer, pos0):
    b, t, _ = za.shape
    tr = min(t, 512)
    nt = t // tr

    def col(c):
        return pl.BlockSpec((1, tr, 1024), lambda bb, i: (bb, i, c))

    return pl.pallas_call(
        functools.partial(_convpool_kernel, tr=tr, nt=nt, pos0=pos0),
        name="convpool",
        grid=(b, nt),
        in_specs=[col(0), col(1), col(2), col(3),
                  pl.BlockSpec((1, 2, D_CONV), lambda bb, i: (bb, 0, 0)),
                  pl.BlockSpec((1, 15, D_POOL), lambda bb, i: (bb, 0, 0)),
                  pl.BlockSpec((None, 3, D_CONV), lambda bb, i: (layer, 0, 0)),
                  pl.BlockSpec((None, 4, POOL_GROUP, POOL_GROUP), lambda bb, i: (layer, 0, 0, 0)),
                  pl.BlockSpec((None, 1, D_POOL), lambda bb, i: (layer, 0, 0))],
        out_specs=[pl.BlockSpec((1, tr, D_CONV), lambda bb, i: (bb, i, 0)),
                   pl.BlockSpec((1, tr, D_POOL), lambda bb, i: (bb, i, 0)),
                   pl.BlockSpec((1, 2, D_CONV), lambda bb, i: (bb, 0, 0)),
                   pl.BlockSpec((1, 15, D_POOL), lambda bb, i: (bb, 0, 0))],
        out_shape=[jax.ShapeDtypeStruct((b, t, D_CONV), BF16),
                   jax.ShapeDtypeStruct((b, t, D_POOL), BF16),
                   jax.ShapeDtypeStruct((b, 2, D_CONV), F32),
                   jax.ShapeDtypeStruct((b, 15, D_POOL), F32)],
        scratch_shapes=[pltpu.VMEM((tr + 8, D_CONV), F32), pltpu.VMEM((tr + 16, D_POOL), F32)],
        compiler_params=_cparams("parallel", "arbitrary"),
    )(za, za, za, za, conv_pre, pool_pre, conv_w, pool_w, pool_scale.reshape(DEPTH, 1, D_POOL))


def _compress_kernel(pt_ref, *refs, n_in, cpr, nsteps, n16):
    del pt_ref
    x_refs = refs[:n_in]
    w1_ref, pe_ref, w2_ref, kc_ref, vc_ref, ab_ref = refs[n_in:]
    s = pl.program_id(1)
    tc = cpr * n_in
    row_w = 2 * N_KV * HEAD_DIM

    @pl.when(s == 0)
    def _():
        ab_ref[:, n16:n16 + 8, :] = jnp.zeros((8, 8, 256), F32)

    for y in range(2):
        parts = []
        for g in range(N_KV):
            c0 = (y * N_KV + g) * HEAD_DIM
            for xr in x_refs:
                parts.append(jnp.concatenate(
                    [xr[0, :, p * row_w + c0:p * row_w + c0 + HEAD_DIM] for p in range(CMP_STRIDE)], axis=1))
        xcat = jnp.concatenate(parts, axis=0).astype(BF16)
        ab = _dot(xcat, w1_ref[y])
        for g in range(N_KV):
            ab_ref[y * N_KV + g, pl.ds(pl.multiple_of(s * tc, 8), tc), :] = ab[g * tc:(g + 1) * tc]

    @pl.when(s == nsteps - 1)
    def _():
        for y in range(2):
            pe = jnp.broadcast_to(pe_ref[y], (8, 2 * CMP_STRIDE * HEAD_DIM)).astype(BF16)
            half = CMP_STRIDE * HEAD_DIM
            c0 = (_dot(pe[:, :half], w1_ref[y, :, 0:HEAD_DIM])
                  + _dot(pe[:, half:], w1_ref[y, :, HEAD_DIM:2 * HEAD_DIM]))[0:1]
            for g in range(N_KV):
                yg = y * N_KV + g
                hid = ab_ref[yg, 0:n16, 0:HEAD_DIM] + ab_ref[yg, 1:n16 + 1, HEAD_DIM:2 * HEAD_DIM] + c0
                o = _dot(jax.nn.gelu(hid).astype(BF16), w2_ref[y])
                if y == 0:
                    kc_ref[0, g] = o.astype(kc_ref.dtype)
                else:
                    vc_ref[0, g] = o.astype(vc_ref.dtype)


def _compress(x_list, x_specs, pt, nb, nsteps, cpr, w1ab, pe, w2, layer):
    n_in = len(x_list)
    n16 = nsteps * n_in * cpr
    grid_spec = pltpu.PrefetchScalarGridSpec(
        num_scalar_prefetch=1,
        grid=(nb, nsteps),
        in_specs=list(x_specs) + [
            pl.BlockSpec((None, 2, 2048, 256), lambda b, s, p: (layer, 0, 0, 0)),
            pl.BlockSpec((None, 2, 1, 4096), lambda b, s, p: (layer, 0, 0, 0)),
            pl.BlockSpec((None, 2, HEAD_DIM, HEAD_DIM), lambda b, s, p: (layer, 0, 0, 0)),
        ],
        out_specs=[pl.BlockSpec((1, N_KV, n16, HEAD_DIM), lambda b, s, p: (b, 0, 0, 0)),
                   pl.BlockSpec((1, N_KV, n16, HEAD_DIM), lambda b, s, p: (b, 0, 0, 0))],
        scratch_shapes=[pltpu.VMEM((8, n16 + 8, 256), F32)],
    )
    return pl.pallas_call(
        functools.partial(_compress_kernel, n_in=n_in, cpr=cpr, nsteps=nsteps, n16=n16),
        name="compress",
        grid_spec=grid_spec,
        out_shape=[jax.ShapeDtypeStruct((nb, N_KV, n16, HEAD_DIM), BF16),
                   jax.ShapeDtypeStruct((nb, N_KV, n16, HEAD_DIM), BF16)],
        compiler_params=_cparams("parallel", "arbitrary"),
    )(pt, *x_list, w1ab, pe, w2)


def _compress_pages_kernel(pt_ref, *refs, n_in, nsteps, n16):
    del pt_ref
    pages = refs[:n_in]
    w1_ref, pe_ref, w2_ref, kc_ref, vc_ref, a_ref, b_ref = refs[n_in:]
    s = pl.program_id(1)
    rpt = 2 * N_KV
    cpp = PAGE // CMP_STRIDE
    rows = n_in * cpp * rpt
    chunk_rows = CMP_STRIDE * rpt

    @pl.when(s == 0)
    def _():
        b_ref[n16 * rpt:(n16 + 8) * rpt, :] = jnp.zeros((8 * rpt, HEAD_DIM), F32)

    x = jnp.concatenate(
        [jnp.concatenate([pg[0, c * chunk_rows + p * rpt:c * chunk_rows + (p + 1) * rpt, :]
                          for pg in pages for c in range(cpp)], axis=0)
         for p in range(CMP_STRIDE)], axis=1).astype(BF16)
    ab = _dot(x, w1_ref[...])
    is_k = (lax.broadcasted_iota(jnp.int32, (rows, 2 * HEAD_DIM), 0) % rpt) < N_KV
    ab = jnp.where(is_k, ab[:, :2 * HEAD_DIM], ab[:, 2 * HEAD_DIM:])
    r0 = pl.multiple_of(s * rows, rows)
    a_ref[pl.ds(r0, rows), :] = ab[:, :HEAD_DIM]
    b_ref[pl.ds(r0, rows), :] = ab[:, HEAD_DIM:]

    @pl.when(s == nsteps - 1)
    def _():
        half = CMP_STRIDE * HEAD_DIM
        for y in range(2):
            pe = jnp.broadcast_to(pe_ref[y], (8, 2 * half)).astype(BF16)
            c0 = (_dot(pe[:, :half], w1_ref[:, y * 256:y * 256 + HEAD_DIM])
                  + _dot(pe[:, half:], w1_ref[:, y * 256 + HEAD_DIM:(y + 1) * 256]))[0:1]
            for g in range(N_KV):
                yg = y * N_KV + g
                hid = (a_ref[pl.ds(yg, n16, stride=rpt), :] + b_ref[pl.ds(rpt + yg, n16, stride=rpt), :] + c0)
                o = _dot(jax.nn.gelu(hid).astype(BF16), w2_ref[y])
                if y == 0:
                    kc_ref[0, g] = o.astype(kc_ref.dtype)
                else:
                    vc_ref[0, g] = o.astype(vc_ref.dtype)


def _compress_pages(cache, page_specs, pt, nb, nsteps, w1cat, pe, w2, layer):
    n_in = len(page_specs)
    n16 = nsteps * n_in * (PAGE // CMP_STRIDE)
    rpt = 2 * N_KV
    grid_spec = pltpu.PrefetchScalarGridSpec(
        num_scalar_prefetch=1,
        grid=(nb, nsteps),
        in_specs=list(page_specs) + [
            pl.BlockSpec((None, 2048, 512), lambda b, s, p: (layer, 0, 0)),
            pl.BlockSpec((None, 2, 1, 4096), lambda b, s, p: (layer, 0, 0, 0)),
            pl.BlockSpec((None, 2, HEAD_DIM, HEAD_DIM), lambda b, s, p: (layer, 0, 0, 0)),
        ],
        out_specs=[pl.BlockSpec((1, N_KV, n16, HEAD_DIM), lambda b, s, p: (b, 0, 0, 0)),
                   pl.BlockSpec((1, N_KV, n16, HEAD_DIM), lambda b, s, p: (b, 0, 0, 0))],
        scratch_shapes=[pltpu.VMEM(((n16 + 8) * rpt, HEAD_DIM), F32), pltpu.VMEM(((n16 + 8) * rpt, HEAD_DIM), F32)],
    )
    return pl.pallas_call(
        functools.partial(_compress_pages_kernel, n_in=n_in, nsteps=nsteps, n16=n16),
        name="compress_pages",
        grid_spec=grid_spec,
        out_shape=[jax.ShapeDtypeStruct((nb, N_KV, n16, HEAD_DIM), BF16),
                   jax.ShapeDtypeStruct((nb, N_KV, n16, HEAD_DIM), BF16)],
        compiler_params=_cparams("parallel", "arbitrary"),
    )(pt, *([cache] * n_in), w1cat, pe, w2)


def _bucket_np(d):
    n = np.maximum(d, 0)
    exact = N_BUCKETS // 2
    nf = np.maximum(n, 1).astype(np.float64)
    frac = np.log(nf / exact) / math.log(MAX_DISTANCE / exact) * (N_BUCKETS - exact)
    near_int = np.abs(frac - np.round(frac)) < 1e-6
    assert not np.any(near_int & (n > exact) & (n < MAX_DISTANCE))
    large = exact + np.floor(frac + 1e-9).astype(np.int64)
    return np.where(n < exact, n, np.minimum(large, N_BUCKETS - 1)).astype(np.int32)


def _idx_np(d, ok):
    return np.where(ok, _bucket_np(d), -1).astype(np.int32)


def _table_kernel(rb_ref, idx_ref, o_ref):
    idx = idx_ref[...]
    for h in range(N_HEADS):
        far = rb_ref[N_BUCKETS - 1, h]
        acc = jnp.zeros(idx.shape, F32)
        for bkt in range(N_BUCKETS - 1):
            acc = jnp.where(idx == bkt, (rb_ref[bkt, h] - far) * LOG2E, acc)
        o_ref[h] = jnp.where(idx < 0, NEG, acc)


def _bias_table(rel_bias, idx):
    r, c = idx.shape
    return pl.pallas_call(
        _table_kernel,
        name="bias_table",
        in_specs=[pl.BlockSpec(memory_space=pltpu.SMEM), pl.BlockSpec((r, c), lambda: (0, 0))],
        out_specs=pl.BlockSpec((N_HEADS, r, c), lambda: (0, 0, 0)),
        out_shape=jax.ShapeDtypeStruct((N_HEADS, r, c), F32),
    )(rel_bias, jnp.asarray(idx))


def _prompt_tables(rel_bias):
    t = np.arange(TQ)[:, None]
    k = np.arange(256)[None, :]
    d = 64 * (3 - k // 64) + t - (k % 64)
    btn = _bias_table(rel_bias, _idx_np(d, d >= 0))
    d = 16 * (12 - k) + t - 31
    cb = _bias_table(rel_bias, np.where(k < 16, _idx_np(d, d >= 0), N_BUCKETS - 1).astype(np.int32))
    g = N_KV
    return btn.reshape(g, Q_PER_KV * TQ, 256), cb.reshape(g, Q_PER_KV * TQ, 256)


def _window_mid_mask_np():
    t = np.arange(TQ)[:, None]
    k = np.arange(512)[None, :]
    d = 64 * (11 - k // 64) + t - (k % 64)
    m = np.where(d < WINDOW, 0.0, NEG).astype(np.float32)
    return np.tile(m, (Q_PER_KV, 1))


def _sel_matrix_np(n_slc_pad, nc_pad, nc):
    j = np.arange(n_slc_pad)[:, None]
    c = np.arange(nc_pad)[None, :]
    return ((c >= 4 * j - 1) & (c <= 4 * j + 3) & (c < nc)).astype(np.float32)


def _online(s, v, m, l, acc):
    m_new = jnp.maximum(m, jnp.max(s, axis=1, keepdims=True))
    a = jnp.exp2(m - m_new)
    p = jnp.exp2(s - m_new)
    l = a * l + jnp.sum(p, axis=1, keepdims=True)
    acc = a * acc + _dot(p.astype(BF16), v)
    return m_new, l, acc


def _first(s, v):
    m = jnp.max(s, axis=1, keepdims=True)
    p = jnp.exp2(s - m)
    return m, jnp.sum(p, axis=1, keepdims=True), _dot(p.astype(BF16), v)


def _online_aug(s, v_aug, m, acc):
    m_new = jnp.maximum(m, jnp.max(s, axis=1, keepdims=True))
    acc = jnp.exp2(m - m_new) * acc + _dot(jnp.exp2(s - m_new).astype(BF16), v_aug)
    return m_new, acc


def _first_aug(s, v_aug):
    m = jnp.max(s, axis=1, keepdims=True)
    return m, _dot(jnp.exp2(s - m).astype(BF16), v_aug)


def _normalise_aug(acc):
    return acc[:, :HEAD_DIM] * (1.0 / acc[:, HEAD_DIM:])


def _rank_select(sc, rivals, jj, cur):
    cnt = jnp.zeros(sc.shape, F32)
    for j2 in range(rivals.shape[0]):
        row = rivals[j2:j2 + 1, :]
        beats = (row > sc) | ((row == sc) & (jj > j2))
        cnt = cnt + jnp.where(beats, 1.0, 0.0)
    return jnp.where((cnt < N_SEL - 0.5) & (jj <= cur), 1.0, 0.0)


def _attn_prompt_kernel(*refs):
    def tile(i, carry):
        _attn_prompt_tile(i, *refs)
        return carry

    lax.fori_loop(0, refs[0].shape[1] // TQ, tile, 0)


def _attn_prompt_tile(i, q_ref, gt_ref, kc_ref, vc_ref, ks_ref, vs_ref, kw_ref, vw_ref,
                      btn_ref, cb_ref, pm_ref, st_ref, et_ref, pad_ref, one_ref, o_ref):
    rows = Q_PER_KV * TQ
    t0 = pl.multiple_of(i * TQ, TQ)
    q = q_ref[0, pl.ds(t0, TQ), :]
    qg = jnp.concatenate([q[:, r * HEAD_DIM:(r + 1) * HEAD_DIM] for r in range(Q_PER_KV)], axis=0)

    lane = lax.broadcasted_iota(jnp.int32, (rows, 256), 1)
    cbr = pltpu.roll(cb_ref[0], lax.rem(4 * i + 244, 256), 1)
    s = _dot_nt(qg, kc_ref[0, 0]) * C1 + jnp.where(lane >= 4 * i + 4, NEG, cbr)
    mx = jnp.max(s, axis=1, keepdims=True)
    mx = jnp.where(mx > 0.5 * NEG, mx, 0.0)
    e = jnp.exp2(s - mx)
    p = e * (1.0 / jnp.maximum(jnp.sum(e, axis=1, keepdims=True), 1e-30))
    o_c = _dot(p.astype(BF16), vc_ref[0, 0])

    imp = p[0:TQ] + p[TQ:2 * TQ] + p[2 * TQ:3 * TQ] + p[3 * TQ:4 * TQ]
    st = st_ref[...]
    h1, h2, h3 = _split3(imp)
    ps_t = _dot_nt(st, h1) + _dot_nt(st, h2) + _dot_nt(st, h3)
    jj = lax.broadcasted_iota(jnp.int32, (64, TQ), 0)
    forced = (jj == 0) | (jj == i) | (jj == i - 1)
    sc = jnp.where(jj > i, NEG_BIG, jnp.where(forced, POS_BIG, ps_t))
    sel_t = _rank_select(sc, sc, jj, i)
    eye = jnp.where(lax.broadcasted_iota(jnp.int32, (TQ, TQ), 0) == lax.broadcasted_iota(jnp.int32, (TQ, TQ), 1),
                    1.0, 0.0).astype(BF16)
    sel_t = jnp.concatenate([sel_t, jnp.zeros((64, TQ), F32)], axis=0).astype(BF16)
    sel = _dot_nt(eye, sel_t)

    jl = lax.broadcasted_iota(jnp.int32, (TQ, 128), 1)

    def mask_lanes(keep):
        m = jnp.where(jl <= 64, 1.0 - keep, 0.0) * NEG
        return jnp.concatenate([m] * Q_PER_KV, axis=0).astype(BF16)

    q_near = jnp.concatenate([qg, mask_lanes(sel)], axis=1)
    q_far = jnp.concatenate([qg, mask_lanes(jnp.where(jl <= i - 4, sel, 0.0))], axis=1)

    def keys(ref, base, n, id_ref):
        return jnp.concatenate([ref[0, pl.ds(base, n), :], id_ref[pl.ds(base, n), :]], axis=1)

    def vals(ref, base, n):
        return jnp.concatenate([ref[0, pl.ds(base, n), :], one_ref[0:n, :]], axis=1)

    btn = btn_ref[0]
    base_n = pl.multiple_of(KV_PAD + (i - 3) * 64, 64)
    base_m = pl.multiple_of(KV_PAD + (i - 11) * 64, 64)

    s = _dot_nt(q_near, jnp.concatenate([keys(ks_ref, base_n, 256, et_ref), keys(kw_ref, base_m, 768, pad_ref)],
                                        axis=0)) * C1
    m_s, a_s = _first_aug(s[:, 0:256] + btn, vals(vs_ref, base_n, 256))
    _, a_w = _first_aug(s[:, 256:1024] + jnp.concatenate([pm_ref[...], btn], axis=1), vals(vw_ref, base_m, 768))
    o_w = _normalise_aug(a_w)


    def far_scores(c):
        base = pl.multiple_of(KV_PAD + c * 512, 512)
        return _dot_nt(q_far, keys(ks_ref, base, 512, et_ref)) * C1, vals(vs_ref, base, 512)

    def far_pair(c2, carry):
        m, acc = carry
        s0, v0 = far_scores(2 * c2)
        s1, v1 = far_scores(2 * c2 + 1)
        m_new = jnp.maximum(m, jnp.maximum(jnp.max(s0, axis=1, keepdims=True), jnp.max(s1, axis=1, keepdims=True)))
        acc = (jnp.exp2(m - m_new) * acc + _dot(jnp.exp2(s0 - m_new).astype(BF16), v0)
               + _dot(jnp.exp2(s1 - m_new).astype(BF16), v1))
        return m_new, acc

    def far_single(carry):
        s0, v0 = far_scores(n_far - 1)
        return _online_aug(s0, v0, *carry)

    n_far = jnp.where(i >= 4, (i - 4) // 8 + 1, 0)
    m_s, a_s = lax.fori_loop(0, n_far // 2, far_pair, (m_s, a_s))
    m_s, a_s = lax.cond(n_far % 2 == 1, far_single, lambda carry: carry, (m_s, a_s))
    o_s = _normalise_aug(a_s)

    gt = jax.nn.sigmoid(gt_ref[0, 0, pl.ds(t0, TQ), :])

    def gcol(br):
        return jnp.concatenate([gt[:, 3 * r + br:3 * r + br + 1] for r in range(Q_PER_KV)], axis=0)

    out = gcol(0) * o_c + gcol(1) * o_s + gcol(2) * o_w
    for r in range(Q_PER_KV):
        o_ref[0, pl.ds(t0, TQ), r * HEAD_DIM:(r + 1) * HEAD_DIM] = out[r * TQ:(r + 1) * TQ].astype(o_ref.dtype)


def _attn_prompt(q, gt, kc, vc, ks, kw, btn, cb, pm, st):
    b, t, _ = q.shape
    lp = ks.shape[1]
    nc = kc.shape[2]
    assert nc == 256 and t // SEL_BLOCK == 64
    kspec = pl.BlockSpec((1, lp, HEAD_DIM), lambda bb, g: (bb, 0, g))
    vspec = pl.BlockSpec((1, lp, HEAD_DIM), lambda bb, g: (bb, 0, N_KV + g))
    row = np.arange(lp)[:, None]
    lane = np.arange(128)[None, :]
    et = jnp.asarray(np.where(row >= KV_PAD, lane == (row - KV_PAD) // SEL_BLOCK, lane == 64).astype(np.float32), BF16)
    pad = jnp.asarray(((row < KV_PAD) & (lane == 64)).astype(np.float32), BF16)
    ones = jnp.ones((768, 128), BF16)
    return pl.pallas_call(
        _attn_prompt_kernel,
        name="attn_prompt",
        grid=(b, N_KV),
        in_specs=[
            pl.BlockSpec((1, t, Q_PER_KV * HEAD_DIM), lambda bb, g: (bb, 0, g)),
            pl.BlockSpec((1, 1, t, 12), lambda bb, g: (bb, g, 0, 0)),
            pl.BlockSpec((1, 1, nc, HEAD_DIM), lambda bb, g: (bb, g, 0, 0)),
            pl.BlockSpec((1, 1, nc, HEAD_DIM), lambda bb, g: (bb, g, 0, 0)),
            kspec, vspec, kspec, vspec,
            pl.BlockSpec((1, Q_PER_KV * TQ, 256), lambda bb, g: (g, 0, 0)),
            pl.BlockSpec((1, Q_PER_KV * TQ, 256), lambda bb, g: (g, 0, 0)),
            pl.BlockSpec((Q_PER_KV * TQ, 512), lambda bb, g: (0, 0)),
            pl.BlockSpec((64, 256), lambda bb, g: (0, 0)),
            pl.BlockSpec((lp, 128), lambda bb, g: (0, 0)),
            pl.BlockSpec((lp, 128), lambda bb, g: (0, 0)),
            pl.BlockSpec((768, 128), lambda bb, g: (0, 0)),
        ],
        out_specs=pl.BlockSpec((1, t, Q_PER_KV * HEAD_DIM), lambda bb, g: (bb, 0, g)),
        out_shape=jax.ShapeDtypeStruct((b, t, N_HEADS * HEAD_DIM), BF16),
        compiler_params=_cparams("parallel", "parallel"),
    )(q, gt, kc, vc, ks, ks, kw, kw, btn, cb, pm, st, et, pad, ones)


S_DEC = 8
S_ROWS = N_HEADS * S_DEC
PAGES_PER_STEP = 8
N_PAGES = PAST_LEN // PAGE
N_STEPS = N_PAGES // PAGES_PER_STEP
NC_S = PAST_LEN // CMP_STRIDE
CUR_S = PAST_LEN // SEL_BLOCK
NJ_S = 264
GW = N_KV * HEAD_DIM


def _diag_blocks(x):
    return jnp.concatenate([x[g * 32:(g + 1) * 32, g * HEAD_DIM:(g + 1) * HEAD_DIM] for g in range(N_KV)], axis=0)


def _attn_sample_kernel(pt_ref, q_ref, gt_ref, kc_ref, vc_ref, *refs):
    del pt_ref
    pages = refs[:PAGES_PER_STEP]
    (snew_ref, wbuf_ref, wnew_ref, tc_ref, tl_ref, tn_ref, tw_ref, st_ref, e16_ref,
     o_ref, qb_ref, sc_ref, sel_ref, m_ref, l_ref, acc_ref, oc_ref) = refs[PAGES_PER_STEP:]
    step = pl.program_id(1)

    @pl.when(step == 0)
    def _():
        q = q_ref[0].astype(F32)
        zero = jnp.zeros((32, HEAD_DIM), F32)
        qgs, qb_rows = [], []
        for g in range(N_KV):
            qg = jnp.concatenate([q[:, (4 * g + r) * HEAD_DIM:(4 * g + r + 1) * HEAD_DIM]
                                  for r in range(Q_PER_KV)], axis=0)
            qgs.append(qg.astype(BF16))
            qb_rows.append(jnp.concatenate([qg if gg == g else zero for gg in range(N_KV)], axis=1))
        qb = jnp.concatenate(qb_rows, axis=0).astype(BF16)
        qb_ref[...] = qb

        s = jnp.concatenate([_dot_nt(qgs[g], kc_ref[0, g]) for g in range(N_KV)], axis=0) * C1 + tc_ref[...]
        mx = jnp.max(s, axis=1, keepdims=True)
        mx = jnp.where(mx > 0.5 * NEG, mx, 0.0)
        e = jnp.exp2(s - mx)
        p = e * (1.0 / jnp.maximum(jnp.sum(e, axis=1, keepdims=True), 1e-30))
        pb = p.astype(BF16)
        oc_ref[...] = jnp.concatenate([_dot(pb[g * 32:(g + 1) * 32], vc_ref[0, g]) for g in range(N_KV)], axis=0)

        reps = []
        for g in range(N_KV):
            pg = p[g * 32:(g + 1) * 32]
            ig = pg[0:8] + pg[8:16] + pg[16:24] + pg[24:32]
            reps += [ig] * Q_PER_KV
        imp = jnp.concatenate(reps, axis=0)
        st = st_ref[...]
        h1, h2, h3 = _split3(imp)
        ps_t = _dot_nt(st, h1) + _dot_nt(st, h2) + _dot_nt(st, h3)
        jj = lax.broadcasted_iota(jnp.int32, (NJ_S, S_ROWS), 0)
        forced = (jj == 0) | (jj == CUR_S) | (jj == CUR_S - 1)
        sc = jnp.where(jj > CUR_S, NEG_BIG, jnp.where(forced, POS_BIG, ps_t))
        sc_ref[...] = sc
        sel_ref[...] = jnp.zeros((NJ_S, S_ROWS), F32)
        jf = jj.astype(F32)

        def take_best(_, carry):
            v = sc_ref[...]
            best = jnp.max(v, axis=0, keepdims=True)
            first = jnp.min(jnp.where(v == best, jf, float(NJ_S)), axis=0, keepdims=True)
            hit = jf == first
            sel_ref[...] = jnp.where(hit, 1.0, sel_ref[...])
            sc_ref[...] = jnp.where(hit, 2.0 * NEG_BIG, v)
            return carry

        lax.fori_loop(0, N_SEL, take_best, 0)
        sel_ref[...] = jnp.where(jj <= CUR_S, sel_ref[...], 0.0)

        new = jnp.concatenate([snew_ref[0], jnp.zeros((128 - S_DEC, 2 * GW), F32)], axis=0).astype(BF16)
        sn = _dot_nt(qb, new[:, :GW]) * C1 + tn_ref[...]
        m0, l0, a0 = _first(sn, new[:, GW:])
        m_ref[...] = m0
        l_ref[...] = l0
        acc_ref[...] = a0

    def gather_rows(refs, n_tok, first):
        return jnp.concatenate(
            [jnp.concatenate([r[0, pl.ds(first + g, n_tok, stride=2 * N_KV), :] for r in refs], axis=0)
             for g in range(N_KV)], axis=1)

    kb = gather_rows(pages, PAGE, 0).astype(BF16)
    vb = gather_rows(pages, PAGE, N_KV).astype(BF16)
    qb = qb_ref[...]
    s = _dot_nt(qb, kb) * C1 + jnp.where(step == N_STEPS - 1, 1.0, 0.0) * tl_ref[...]
    blocks = PAGES_PER_STEP * PAGE // SEL_BLOCK
    sel_step = sel_ref[pl.ds(pl.multiple_of(step * blocks, blocks), blocks), :]
    eye = jnp.where(lax.broadcasted_iota(jnp.int32, (S_ROWS, S_ROWS), 0)
                    == lax.broadcasted_iota(jnp.int32, (S_ROWS, S_ROWS), 1), 1.0, 0.0).astype(BF16)
    sel_r = _dot_nt(eye, sel_step.astype(BF16)).astype(BF16)
    s = jnp.where(_dot(sel_r, e16_ref[...]) > 0.5, s, NEG)
    m1, l1, a1 = _online(s, vb, m_ref[...], l_ref[...], acc_ref[...])
    m_ref[...] = m1
    l_ref[...] = l1
    acc_ref[...] = a1

    @pl.when(step == N_STEPS - 1)
    def _():
        o_s = _diag_blocks(a1 * (1.0 / l1))
        wn = jnp.concatenate([wnew_ref[0], jnp.zeros((128 - S_DEC, 2 * GW), F32)], axis=0)
        wk = jnp.concatenate([gather_rows([wbuf_ref], WINDOW, 0), wn[:, :GW]], axis=0).astype(BF16)
        wv = jnp.concatenate([gather_rows([wbuf_ref], WINDOW, N_KV), wn[:, GW:]], axis=0).astype(BF16)
        sw = _dot_nt(qb, wk) * C1 + tw_ref[...]
        _, lw, aw = _first(sw, wv)
        o_w = _diag_blocks(aw * (1.0 / lw))
        gt = jax.nn.sigmoid(gt_ref[0])
        out = gt[:, 0:1] * oc_ref[...] + gt[:, 1:2] * o_s + gt[:, 2:3] * o_w
        for h in range(N_HEADS):
            o_ref[0, :, h * HEAD_DIM:(h + 1) * HEAD_DIM] = out[h * S_DEC:(h + 1) * S_DEC]


def _attn_sample(q, gt, kc, vc, cache, page_table, page_base, snew, wbuf, wnew, layer, tabs):
    b = q.shape[0]
    tc, tl, tn, tw, st, e16 = tabs
    rows_per_tok = 2 * N_KV

    def page_spec(j):
        return pl.BlockSpec((1, PAGE * rows_per_tok, HEAD_DIM),
                            lambda bb, s, pt: (page_base + pt[bb, s * PAGES_PER_STEP + j], 0, 0))

    def full(shape):
        return pl.BlockSpec(shape, lambda bb, s, pt: tuple(0 for _ in shape))

    grid_spec = pltpu.PrefetchScalarGridSpec(
        num_scalar_prefetch=1,
        grid=(b, N_STEPS),
        in_specs=[
            pl.BlockSpec((1, S_DEC, N_HEADS * HEAD_DIM), lambda bb, s, pt: (bb, 0, 0)),
            pl.BlockSpec((1, S_ROWS, 3), lambda bb, s, pt: (bb, 0, 0)),
            pl.BlockSpec((1, N_KV, NC_S, HEAD_DIM), lambda bb, s, pt: (bb, 0, 0, 0)),
            pl.BlockSpec((1, N_KV, NC_S, HEAD_DIM), lambda bb, s, pt: (bb, 0, 0, 0)),
        ] + [page_spec(j) for j in range(PAGES_PER_STEP)] + [
            pl.BlockSpec((1, S_DEC, 2 * GW), lambda bb, s, pt: (bb, 0, 0)),
            pl.BlockSpec((None, 1, WINDOW * rows_per_tok, HEAD_DIM), lambda bb, s, pt: (layer, bb, 0, 0)),
            pl.BlockSpec((1, S_DEC, 2 * GW), lambda bb, s, pt: (bb, 0, 1)),
            full(tc.shape), full(tl.shape), full(tn.shape), full(tw.shape), full(st.shape), full(e16.shape),
        ],
        out_specs=pl.BlockSpec((1, S_DEC, N_HEADS * HEAD_DIM), lambda bb, s, pt: (bb, 0, 0)),
        scratch_shapes=[pltpu.VMEM((S_ROWS, GW), BF16), pltpu.VMEM((NJ_S, S_ROWS), F32),
                        pltpu.VMEM((NJ_S, S_ROWS), F32), pltpu.VMEM((S_ROWS, 1), F32), pltpu.VMEM((S_ROWS, 1), F32),
                        pltpu.VMEM((S_ROWS, GW), F32), pltpu.VMEM((S_ROWS, HEAD_DIM), F32)],
    )
    return pl.pallas_call(
        _attn_sample_kernel,
        name="attn_sample",
        grid_spec=grid_spec,
        out_shape=jax.ShapeDtypeStruct((b, S_DEC, N_HEADS * HEAD_DIM), F32),
        compiler_params=_cparams("parallel", "arbitrary"),
    )(page_table, q, gt, kc, vc, *([cache] * PAGES_PER_STEP), snew, wbuf, wnew, tc, tl, tn, tw, st, e16)


def _sample_tables(rel_bias):
    s = np.arange(S_DEC)[:, None]
    qpos = PAST_LEN + s

    def rows(tab):
        return tab.reshape(S_ROWS, tab.shape[-1])

    c = np.arange(NC_S)[None, :]
    d = qpos - (CMP_STRIDE * c + 2 * CMP_STRIDE - 1)
    tc = rows(_bias_table(rel_bias, _idx_np(d, d >= 0)))
    k = np.arange(PAGES_PER_STEP * PAGE)[None, :]
    d = qpos - (PAST_LEN - PAGES_PER_STEP * PAGE + k)
    tl = rows(_bias_table(rel_bias, _idx_np(d, d >= 0)))
    k = np.arange(128)[None, :]
    d = s - k
    tn = rows(_bias_table(rel_bias, _idx_np(d, (d >= 0) & (k < S_DEC))))
    k = np.arange(WINDOW + 128)[None, :]
    d = np.where(k < WINDOW, qpos - (PAST_LEN - WINDOW + k), s - (k - WINDOW))
    tw = rows(_bias_table(rel_bias, _idx_np(d, (d >= 0) & (d < WINDOW) & (k < WINDOW + S_DEC))))
    st = jnp.asarray(_sel_matrix_np(NJ_S, NC_S, NC_S - 1), BF16)
    blocks = PAGES_PER_STEP * PAGE // SEL_BLOCK
    e16 = jnp.asarray((np.arange(blocks)[:, None] == np.arange(PAGES_PER_STEP * PAGE)[None, :] // SEL_BLOCK)
                      .astype(np.float32), BF16)
    return tc, tl, tn, tw, st, e16


def _projections(hf, w_main, w_gn, w_gm, layer):
    za = _mm(hf, w_main, layer, 0, 4096, F32, w_t=True)
    q = _mm(hf, w_main, layer, 4096, 2048, BF16, w_t=True)
    cmp = _mm(hf, w_main, layer, 6144, 1024, F32, w_t=True)
    sw = _mm(hf, w_main, layer, 7168, 2048, F32, w_t=True)
    gn = _mm(hf, w_gn, layer, 0, 48, F32, w_t=True)
    gm = _mm(hf, w_gm, layer, N_MAIN + 48, 3 * D_MODEL, F32, w_t=True)
    return za, q, cmp, sw, gn, gm


def kernel(x_prompt, x_sample, cache_cmp_kv, cache_slc_kv, state_win_kv, state_conv, state_pool, page_table,
           c_prompt, c_sample, rel_bias, norm_g, ada_w, ada_b, w_in, conv_w, pool_w, pool_scale,
           cmp_pe, cmp_w1, cmp_w2, w_branch, w_out, mlp_w1, mlp_w2):
    bp, t, d = x_prompt.shape
    bs, s_len, _ = x_sample.shape
    n_phys = cache_cmp_kv.shape[1]
    assert (t, d, s_len, bs) == (4096, D_MODEL, S_DEC, 8)

    w_main = jnp.swapaxes(w_in, 1, 2)
    w_gn = w_main[:, N_MAIN:N_MAIN + 48]
    w_gm = w_main
    ada_wb = ada_w
    w_branch_b = w_branch.astype(BF16)
    w_out_b = w_out
    w1_b = mlp_w1
    w2_b = mlp_w2
    pool_wb = pool_w.astype(BF16)
    w1r = cmp_w1.reshape(DEPTH, 2, 2 * CMP_STRIDE, HEAD_DIM, HEAD_DIM)
    half = CMP_STRIDE * HEAD_DIM
    w1ab = jnp.concatenate([w1r[:, :, :CMP_STRIDE].reshape(DEPTH, 2, half, HEAD_DIM),
                            w1r[:, :, CMP_STRIDE:].reshape(DEPTH, 2, half, HEAD_DIM)], axis=-1).astype(BF16)
    w1cat = jnp.concatenate([w1ab[:, 0], w1ab[:, 1]], axis=-1)
    pe = cmp_pe.reshape(DEPTH, 2, 1, 2 * half)
    w2c = cmp_w2.astype(BF16)

    c16 = jnp.concatenate([c_prompt, c_sample, jnp.zeros((16 - bp - bs, d), F32)], axis=0)
    mod = _ada(c16, ada_wb, ada_b)
    mod_p = mod[:, :bp].reshape(DEPTH, bp, 6, d)
    mod_s = mod[:, bp:bp + bs].reshape(DEPTH, bs, 6, d)

    btn, cb = _prompt_tables(rel_bias)
    pm = jnp.asarray(_window_mid_mask_np())
    st_p = jnp.asarray(_sel_matrix_np(t // SEL_BLOCK, t // CMP_STRIDE, t // CMP_STRIDE - 1), BF16)
    tabs_s = _sample_tables(rel_bias)

    rpt = 2 * N_KV
    cache_c = cache_cmp_kv.reshape(DEPTH * n_phys, PAGE * rpt, HEAD_DIM)
    cache_s = cache_slc_kv.reshape(DEPTH * n_phys, PAGE * rpt, HEAD_DIM)
    wbuf = state_win_kv.reshape(DEPTH, bs, WINDOW * rpt, HEAD_DIM)
    conv0 = jnp.zeros((bp, 2, D_CONV), F32)
    pool0 = jnp.zeros((bp, 15, D_POOL), F32)
    dummy_pt = jnp.zeros((1, 1), jnp.int32)

    xp, xs = x_prompt, x_sample
    _, hp = _norm(xp, None, norm_g, mod_p, None, (0, 0, 0, 1))
    _, hs = _norm(xs, None, norm_g, mod_s, None, (0, 0, 0, 1))
    outs = [[] for _ in range(10)]
    mp, ms = bp * t, bs * s_len
    for l in range(DEPTH):
        za, q, cmp, sw, gn, gm = _projections(hp.reshape(mp, d), w_main, w_gn, w_gm, l)
        ya, yb, cst_p, pst_p = _convpool(za.reshape(bp, t, 4096), conv0, pool0, conv_w, pool_wb, pool_scale, l, 0)
        n16 = t // CMP_STRIDE
        kc, vc = _compress([cmp.reshape(bp, n16, CMP_STRIDE * 2 * GW)],
                           [pl.BlockSpec((1, 64, CMP_STRIDE * 2 * GW), lambda b, s, p: (b, s, 0))],
                           dummy_pt, bp, n16 // 64, 64, w1ab, pe, w2c, l)
        swp = sw.reshape(bp, t, 4 * GW)
        ks = jnp.pad(swp[:, :, :2 * GW].astype(BF16), ((0, 0), (KV_PAD, 0), (0, 0)))
        kw = jnp.pad(swp[:, :, 2 * GW:].astype(BF16), ((0, 0), (KV_PAD, 0), (0, 0)))
        gt = gn.reshape(bp, t, N_KV, 12).transpose(0, 2, 1, 3)
        yc = _attn_prompt(q.reshape(bp, t, 2048), gt, kc, vc, ks, kw, btn, cb, pm, st_p)
        merged = _merge(ya.reshape(mp, D_CONV), yb.reshape(mp, D_POOL), yc.reshape(mp, 2048), gm, w_branch_b, l)
        mo = _mm(merged, w_out_b, l, 0, d, F32)
        xp, h2 = _norm(xp, mo.reshape(bp, t, d), norm_g, mod_p, (l, 1, 2), (l, 2, 3, 4))
        u = _mm(h2.reshape(mp, d), w1_b, l, 0, D_FF, BF16, act="relu2")
        f = _mm(u, w2_b, l, 0, d, F32)
        xp, hp = _norm(xp, f.reshape(bp, t, d), norm_g, mod_p, (l, 3, 5), (l + 1, 0, 0, 1) if l + 1 < DEPTH else None)
        sw5 = swp.reshape(bp, t, 2, 2, N_KV, HEAD_DIM)
        outs[0].append(cmp.reshape(bp, t, 2, N_KV, HEAD_DIM))
        outs[2].append(sw5[:, :, 0])
        outs[4].append(sw5[:, t - WINDOW:, 1])
        outs[6].append(cst_p)
        outs[8].append(pst_p)

        za, q, cmp, sw, gn, gm = _projections(hs.reshape(ms, d), w_main, w_gn, w_gm, l)
        ya, yb, cst_s, pst_s = _convpool(za.reshape(bs, s_len, 4096), state_conv[l], state_pool[l],
                                         conv_w, pool_wb, pool_scale, l, PAST_LEN)
        sws = sw.reshape(bs, s_len, 4 * GW)
        page_specs = [pl.BlockSpec((1, PAGE * rpt, HEAD_DIM),
                                   lambda b, s, p, j=j, l=l: (l * n_phys + p[b, s * PAGES_PER_STEP + j], 0, 0))
                      for j in range(PAGES_PER_STEP)]
        kc, vc = _compress_pages(cache_c, page_specs, page_table, bs, N_STEPS, w1cat, pe, w2c, l)
        gts = gn.reshape(bs, s_len, N_KV, Q_PER_KV, 3).transpose(0, 2, 3, 1, 4).reshape(bs, S_ROWS, 3)
        yc = _attn_sample(q.reshape(bs, s_len, 2048), gts, kc, vc, cache_s, page_table, l * n_phys,
                          sws, wbuf, sws, l, tabs_s)
        merged = _merge(ya.reshape(ms, D_CONV), yb.reshape(ms, D_POOL), yc.reshape(ms, 2048), gm, w_branch_b, l)
        mo = _mm(merged, w_out_b, l, 0, d, F32)
        xs, h2 = _norm(xs, mo.reshape(bs, s_len, d), norm_g, mod_s, (l, 1, 2), (l, 2, 3, 4))
        u = _mm(h2.reshape(ms, d), w1_b, l, 0, D_FF, BF16, act="relu2")
        f = _mm(u, w2_b, l, 0, d, F32)
        xs, hs = _norm(xs, f.reshape(bs, s_len, d), norm_g, mod_s, (l, 3, 5), (l + 1, 0, 0, 1) if l + 1 < DEPTH else None)
        sw5 = sws.reshape(bs, s_len, 2, 2, N_KV, HEAD_DIM)
        outs[1].append(cmp.reshape(bs, s_len, 2, N_KV, HEAD_DIM))
        outs[3].append(sw5[:, :, 0])
        outs[5].append(jnp.concatenate([state_win_kv[l][:, s_len:], sw5[:, :, 1]], axis=1))
        outs[7].append(cst_s)
        outs[9].append(pst_s)

    return (xp, xs) + tuple(jnp.stack(o) for o in outs)
```

```python
import functools
import math

import numpy as np
import jax
import jax.numpy as jnp
from jax import lax
from jax.experimental import pallas as pl
from jax.experimental.pallas import tpu as pltpu

F32 = jnp.float32
BF16 = jnp.bfloat16

D_MODEL = 2048
DEPTH = 4
PAST_LEN = 16384
PAGE = 128
D_CONV = 1024
D_POOL = 1024
POOL_WINDOWS = (2, 4, 8, 16)
POOL_GROUP = 256
HEAD_DIM = 128
N_KV = 4
Q_PER_KV = 4
N_HEADS = 16
CMP_STRIDE = 16
SEL_BLOCK = 64
N_SEL = 16
WINDOW = 512
N_BUCKETS = 32
MAX_DISTANCE = 128
D_FF = 8192
N_MAIN = 9216
RMS_EPS = 1e-6
NEG_BIG = -1e9
POS_BIG = 1e9
SCALE = HEAD_DIM ** -0.5
LOG2E = math.log2(math.e)
C1 = SCALE * LOG2E

NEG = -1e30
KV_PAD = 1024
VMEM_LIMIT = 56 * 1024 * 1024
TQ = 64


def _cparams(*sem):
    return pltpu.CompilerParams(dimension_semantics=sem, vmem_limit_bytes=VMEM_LIMIT)


def _dot(a, b):
    return jnp.dot(a, b, preferred_element_type=F32)


def _dot_nt(a, b):
    return lax.dot_general(a, b, (((1,), (1,)), ((), ())), preferred_element_type=F32)


def _split3(x):
    h1 = x.astype(BF16)
    r1 = x - h1.astype(F32)
    h2 = r1.astype(BF16)
    h3 = (r1 - h2.astype(F32)).astype(BF16)
    return h1, h2, h3


def _mm_kernel(a_ref, w_ref, o_ref, *scratch, nk, act, stage_w, w_t):
    if len(w_ref.shape) == 3:
        w_ref = w_ref.at[0]
    if stage_w:
        wb_ref = scratch[-1]

        @pl.when(pl.program_id(1) == 0)
        def _():
            wb_ref[...] = w_ref[...].astype(BF16)

        w = wb_ref[...]
    else:
        w = w_ref[...].astype(BF16)
    part = (_dot_nt if w_t else _dot)(a_ref[...].astype(BF16), w)

    def finish(acc):
        if act == "relu2":
            r = jnp.maximum(acc, 0.0)
            acc = r * r
        o_ref[...] = acc.astype(o_ref.dtype)

    if nk == 1:
        finish(part)
    else:
        acc_ref = scratch[0]
        k = pl.program_id(2)

        @pl.when(k == 0)
        def _():
            acc_ref[...] = part

        @pl.when(k > 0)
        def _():
            acc_ref[...] += part

        @pl.when(k == nk - 1)
        def _():
            finish(acc_ref[...])


def _mm(a, w, layer, col0, n, out_dtype, act=None, tn=512, w_t=False):
    m, k = a.shape
    tm = min(m, 1024)
    tk = min(k, 2048)
    tn = min(tn, n)
    nk = k // tk
    assert m % tm == 0 and k % tk == 0 and n % tn == 0 and (col0 % tn == 0 or (w_t and col0 % 8 == 0))
    cb0 = col0 // tn
    stage_w = nk == 1 and m // tm > 1 and w.dtype != BF16
    wblock = (tn, tk) if w_t else (tk, tn)
    scratch = ([pltpu.VMEM((tm, tn), F32)] if nk > 1 else []) + ([pltpu.VMEM(wblock, BF16)] if stage_w else [])
    if w_t and col0 % tn:
        wspec = pl.BlockSpec((pl.Element(1), pl.Element(tn), pl.Element(tk)),
                             lambda j, i, kk: (layer, pl.multiple_of(col0 + j * tn, 8), kk * tk))
    elif w_t:
        wspec = pl.BlockSpec((None, tn, tk), lambda j, i, kk: (layer, cb0 + j, kk))
    else:
        wspec = pl.BlockSpec((None, tk, tn), lambda j, i, kk: (layer, kk, cb0 + j))
    return pl.pallas_call(
        functools.partial(_mm_kernel, nk=nk, act=act, stage_w=stage_w, w_t=w_t),
        name="mm",
        grid=(n // tn, m // tm, nk),
        in_specs=[pl.BlockSpec((tm, tk), lambda j, i, kk: (i, kk)), wspec],
        out_specs=pl.BlockSpec((tm, tn), lambda j, i, kk: (i, j)),
        out_shape=jax.ShapeDtypeStruct((m, n), out_dtype),
        scratch_shapes=scratch,
        compiler_params=_cparams("parallel", "arbitrary", "arbitrary"),
    )(a, w)


def _merge_kernel(ya_ref, yb_ref, yc_ref, g0_ref, g1_ref, g2_ref, wa_ref, wb_ref, wc_ref, o_ref):
    acc = jax.nn.sigmoid(g0_ref[...]) * _dot(ya_ref[...].astype(BF16), wa_ref[...])
    acc += jax.nn.sigmoid(g1_ref[...]) * _dot(yb_ref[...].astype(BF16), wb_ref[...])
    acc += jax.nn.sigmoid(g2_ref[...]) * _dot(yc_ref[...].astype(BF16), wc_ref[...])
    o_ref[...] = acc.astype(o_ref.dtype)


def _merge(ya, yb, yc, gm, w_branch, layer):
    m = ya.shape[0]
    tm = min(m, 1024)
    tn = 512
    nb = D_MODEL // tn
    return pl.pallas_call(
        _merge_kernel,
        name="merge",
        grid=(nb, m // tm),
        in_specs=[
            pl.BlockSpec((tm, D_CONV), lambda j, i: (i, 0)),
            pl.BlockSpec((tm, D_POOL), lambda j, i: (i, 0)),
            pl.BlockSpec((tm, D_MODEL), lambda j, i: (i, 0)),
            pl.BlockSpec((tm, tn), lambda j, i: (i, j)),
            pl.BlockSpec((tm, tn), lambda j, i: (i, nb + j)),
            pl.BlockSpec((tm, tn), lambda j, i: (i, 2 * nb + j)),
            pl.BlockSpec((None, D_CONV, tn), lambda j, i: (layer, 0, j)),
            pl.BlockSpec((None, D_POOL, tn), lambda j, i: (layer, 1, j)),
            pl.BlockSpec((None, D_MODEL, tn), lambda j, i: (layer, 1, j)),
        ],
        out_specs=pl.BlockSpec((tm, tn), lambda j, i: (i, j)),
        out_shape=jax.ShapeDtypeStruct((m, D_MODEL), BF16),
        compiler_params=_cparams("parallel", "parallel"),
    )(ya, yb, yc, gm, gm, gm, w_branch, w_branch, w_branch)


def _ada_kernel(c_ref, w_ref, b_ref, o_ref):
    c = c_ref[...]
    a = (c * jax.nn.sigmoid(c)).astype(BF16)
    o_ref[...] = _dot(a, w_ref[...].astype(BF16)) + b_ref[...]


def _ada(c16, ada_w, ada_b):
    tn = 1024
    n = 6 * D_MODEL
    return pl.pallas_call(
        _ada_kernel,
        name="ada",
        grid=(DEPTH, n // tn),
        in_specs=[
            pl.BlockSpec((16, D_MODEL), lambda l, j: (0, 0)),
            pl.BlockSpec((None, D_MODEL, tn), lambda l, j: (l, 0, j)),
            pl.BlockSpec((None, 1, tn), lambda l, j: (l, 0, j)),
        ],
        out_specs=pl.BlockSpec((None, 16, tn), lambda l, j: (l, 0, j)),
        out_shape=jax.ShapeDtypeStruct((DEPTH, 16, n), F32),
        compiler_params=_cparams("parallel", "parallel"),
    )(c16, ada_w, ada_b.reshape(DEPTH, 1, n))


def _rms(x, g):
    return x * lax.rsqrt(jnp.mean(x * x, axis=-1, keepdims=True) + RMS_EPS) * g


def _norm_kernel(*refs, res, hh):
    it = iter(refs)
    x = next(it)[0]
    if res is not None:
        m_ref, ng_ref, mod_ref = next(it), next(it), next(it)
    if hh is not None:
        ngh_ref, modh_ref = next(it), next(it)
    if res is not None:
        _, ng_row, gate_row = res
        x = x + mod_ref[0, gate_row:gate_row + 1, :] * _rms(m_ref[0], ng_ref[ng_row:ng_row + 1, :])
        next(it)[0] = x
    if hh is not None:
        _, ng_row, shift_row, scale_row = hh
        h = (_rms(x, ngh_ref[ng_row:ng_row + 1, :]) * (1.0 + modh_ref[0, scale_row:scale_row + 1, :])
             + modh_ref[0, shift_row:shift_row + 1, :])
        next(it)[0] = h.astype(BF16)


def _norm(x, m, norm_g, mod, res, hh):
    b, t, d = x.shape
    tr = min(t, 512)
    xspec = pl.BlockSpec((1, tr, d), lambda bb, i: (bb, i, 0))

    def params(layer):
        return ([norm_g, mod],
                [pl.BlockSpec((None, 4, d), lambda bb, i: (layer, 0, 0)),
                 pl.BlockSpec((None, 1, 6, d), lambda bb, i: (layer, bb, 0, 0))])

    ins, specs, outs, ospecs = [x], [xspec], [], []
    if res is not None:
        a, s = params(res[0])
        ins += [m] + a
        specs += [xspec] + s
        outs.append(jax.ShapeDtypeStruct((b, t, d), F32))
        ospecs.append(xspec)
    if hh is not None:
        a, s = params(hh[0])
        ins += a
        specs += s
        outs.append(jax.ShapeDtypeStruct((b, t, d), BF16))
        ospecs.append(xspec)
    out = pl.pallas_call(
        functools.partial(_norm_kernel, res=res, hh=hh),
        name="norm",
        grid=(b, t // tr),
        in_specs=specs,
        out_specs=ospecs,
        out_shape=outs,
        compiler_params=_cparams("parallel", "parallel"),
    )(*ins)
    out = list(out)
    x_new = out.pop(0) if res is not None else None
    h = out.pop(0) if hh is not None else None
    return x_new, h


def _convpool_kernel(ab_ref, ac_ref, ax_ref, p_ref, cpre_ref, ppre_ref, cw_ref, pw_ref, ps_ref,
                     ya_ref, yb_ref, cst_ref, pst_ref, u_ext, p_ext, *, tr, nt, pos0):
    t = pl.program_id(1)

    @pl.when(t == 0)
    def _():
        u_ext[0:8, :] = jnp.zeros((8, D_CONV), F32)
        u_ext[6:8, :] = cpre_ref[0]
        p_ext[0:16, :] = jnp.zeros((16, D_POOL), F32)
        p_ext[1:16, :] = ppre_ref[0]

    u = ac_ref[0] * ax_ref[0]
    u_ext[8:8 + tr, :] = u
    v = cw_ref[0:1, :] * u_ext[6:6 + tr, :] + cw_ref[1:2, :] * u_ext[7:7 + tr, :] + cw_ref[2:3, :] * u
    ya_ref[0] = (ab_ref[0] * v).astype(ya_ref.dtype)

    p = p_ref[0]
    p_ext[16:16 + tr, :] = p
    pos = pos0 + t * tr + lax.broadcasted_iota(jnp.int32, (tr, 1), 0)
    for g, w in enumerate(POOL_WINDOWS):
        lo = g * POOL_GROUP
        s = p[:, lo:lo + POOL_GROUP]
        for k in range(1, w):
            s = s + p_ext[16 - k:16 - k + tr, lo:lo + POOL_GROUP]
        cnt = jnp.minimum(pos + 1, w).astype(F32)
        dlt = s / cnt - p[:, lo:lo + POOL_GROUP]
        y = _dot(dlt.astype(BF16), pw_ref[g])
        yb_ref[0, :, lo:lo + POOL_GROUP] = (y * ps_ref[:, lo:lo + POOL_GROUP]).astype(yb_ref.dtype)

    @pl.when(t == nt - 1)
    def _():
        cst_ref[0] = u_ext[tr + 6:tr + 8, :]
        pst_ref[0] = p_ext[tr + 1:tr + 16, :]

    if nt > 1:
        u_ext[0:8, :] = u_ext[tr:tr + 8, :]
        p_ext[0:16, :] = p_ext[tr:tr + 16, :]


def _convpool(za, conv_pre, pool_pre, conv_w, pool_w, pool_scale, layer, pos0):
    b, t, _ = za.shape
    tr = min(t, 512)
    nt = t // tr

    def col(c):
        return pl.BlockSpec((1, tr, 1024), lambda bb, i: (bb, i, c))

    return pl.pallas_call(
        functools.partial(_convpool_kernel, tr=tr, nt=nt, pos0=pos0),
        name="convpool",
        grid=(b, nt),
        in_specs=[col(0), col(1), col(2), col(3),
                  pl.BlockSpec((1, 2, D_CONV), lambda bb, i: (bb, 0, 0)),
                  pl.BlockSpec((1, 15, D_POOL), lambda bb, i: (bb, 0, 0)),
                  pl.BlockSpec((None, 3, D_CONV), lambda bb, i: (layer, 0, 0)),
                  pl.BlockSpec((None, 4, POOL_GROUP, POOL_GROUP), lambda bb, i: (layer, 0, 0, 0)),
                  pl.BlockSpec((None, 1, D_POOL), lambda bb, i: (layer, 0, 0))],
        out_specs=[pl.BlockSpec((1, tr, D_CONV), lambda bb, i: (bb, i, 0)),
                   pl.BlockSpec((1, tr, D_POOL), lambda bb, i: (bb, i, 0)),
                   pl.BlockSpec((1, 2, D_CONV), lambda bb, i: (bb, 0, 0)),
                   pl.BlockSpec((1, 15, D_POOL), lambda bb, i: (bb, 0, 0))],
        out_shape=[jax.ShapeDtypeStruct((b, t, D_CONV), BF16),
                   jax.ShapeDtypeStruct((b, t, D_POOL), BF16),
                   jax.ShapeDtypeStruct((b, 2, D_CONV), F32),
                   jax.ShapeDtypeStruct((b, 15, D_POOL), F32)],
        scratch_shapes=[pltpu.VMEM((tr + 8, D_CONV), F32), pltpu.VMEM((tr + 16, D_POOL), F32)],
        compiler_params=_cparams("parallel", "arbitrary"),
    )(za, za, za, za, conv_pre, pool_pre, conv_w, pool_w, pool_scale.reshape(DEPTH, 1, D_POOL))


def _compress_kernel(pt_ref, *refs, n_in, cpr, nsteps, n16):
    del pt_ref
    x_refs = refs[:n_in]
    w1_ref, pe_ref, w2_ref, kc_ref, vc_ref, ab_ref = refs[n_in:]
    s = pl.program_id(1)
    tc = cpr * n_in
    row_w = 2 * N_KV * HEAD_DIM

    @pl.when(s == 0)
    def _():
        ab_ref[:, n16:n16 + 8, :] = jnp.zeros((8, 8, 256), F32)

    for y in range(2):
        parts = []
        for g in range(N_KV):
            c0 = (y * N_KV + g) * HEAD_DIM
            for xr in x_refs:
                parts.append(jnp.concatenate(
                    [xr[0, :, p * row_w + c0:p * row_w + c0 + HEAD_DIM] for p in range(CMP_STRIDE)], axis=1))
        xcat = jnp.concatenate(parts, axis=0).astype(BF16)
        ab = _dot(xcat, w1_ref[y])
        for g in range(N_KV):
            ab_ref[y * N_KV + g, pl.ds(pl.multiple_of(s * tc, 8), tc), :] = ab[g * tc:(g + 1) * tc]

    @pl.when(s == nsteps - 1)
    def _():
        for y in range(2):
            pe = jnp.broadcast_to(pe_ref[y], (8, 2 * CMP_STRIDE * HEAD_DIM)).astype(BF16)
            half = CMP_STRIDE * HEAD_DIM
            c0 = (_dot(pe[:, :half], w1_ref[y, :, 0:HEAD_DIM])
                  + _dot(pe[:, half:], w1_ref[y, :, HEAD_DIM:2 * HEAD_DIM]))[0:1]
            for g in range(N_KV):
                yg = y * N_KV + g
                hid = ab_ref[yg, 0:n16, 0:HEAD_DIM] + ab_ref[yg, 1:n16 + 1, HEAD_DIM:2 * HEAD_DIM] + c0
                o = _dot(jax.nn.gelu(hid).astype(BF16), w2_ref[y])
                if y == 0:
                    kc_ref[0, g] = o.astype(kc_ref.dtype)
                else:
                    vc_ref[0, g] = o.astype(vc_ref.dtype)


def _compress(x_list, x_specs, pt, nb, nsteps, cpr, w1ab, pe, w2, layer):
    n_in = len(x_list)
    n16 = nsteps * n_in * cpr
    grid_spec = pltpu.PrefetchScalarGridSpec(
        num_scalar_prefetch=1,
        grid=(nb, nsteps),
        in_specs=list(x_specs) + [
            pl.BlockSpec((None, 2, 2048, 256), lambda b, s, p: (layer, 0, 0, 0)),
            pl.BlockSpec((None, 2, 1, 4096), lambda b, s, p: (layer, 0, 0, 0)),
            pl.BlockSpec((None, 2, HEAD_DIM, HEAD_DIM), lambda b, s, p: (layer, 0, 0, 0)),
        ],
        out_specs=[pl.BlockSpec((1, N_KV, n16, HEAD_DIM), lambda b, s, p: (b, 0, 0, 0)),
                   pl.BlockSpec((1, N_KV, n16, HEAD_DIM), lambda b, s, p: (b, 0, 0, 0))],
        scratch_shapes=[pltpu.VMEM((8, n16 + 8, 256), F32)],
    )
    return pl.pallas_call(
        functools.partial(_compress_kernel, n_in=n_in, cpr=cpr, nsteps=nsteps, n16=n16),
        name="compress",
        grid_spec=grid_spec,
        out_shape=[jax.ShapeDtypeStruct((nb, N_KV, n16, HEAD_DIM), BF16),
                   jax.ShapeDtypeStruct((nb, N_KV, n16, HEAD_DIM), BF16)],
        compiler_params=_cparams("parallel", "arbitrary"),
    )(pt, *x_list, w1ab, pe, w2)


def _compress_pages_kernel(pt_ref, *refs, n_in, nsteps, n16):
    del pt_ref
    pages = refs[:n_in]
    w1_ref, pe_ref, w2_ref, kc_ref, vc_ref, a_ref, b_ref = refs[n_in:]
    s = pl.program_id(1)
    rpt = 2 * N_KV
    cpp = PAGE // CMP_STRIDE
    rows = n_in * cpp * rpt
    chunk_rows = CMP_STRIDE * rpt

    @pl.when(s == 0)
    def _():
        b_ref[n16 * rpt:(n16 + 8) * rpt, :] = jnp.zeros((8 * rpt, HEAD_DIM), F32)

    x = jnp.concatenate(
        [jnp.concatenate([pg[0, c * chunk_rows + p * rpt:c * chunk_rows + (p + 1) * rpt, :]
                          for pg in pages for c in range(cpp)], axis=0)
         for p in range(CMP_STRIDE)], axis=1).astype(BF16)
    ab = _dot(x, w1_ref[...])
    is_k = (lax.broadcasted_iota(jnp.int32, (rows, 2 * HEAD_DIM), 0) % rpt) < N_KV
    ab = jnp.where(is_k, ab[:, :2 * HEAD_DIM], ab[:, 2 * HEAD_DIM:])
    r0 = pl.multiple_of(s * rows, rows)
    a_ref[pl.ds(r0, rows), :] = ab[:, :HEAD_DIM]
    b_ref[pl.ds(r0, rows), :] = ab[:, HEAD_DIM:]

    @pl.when(s == nsteps - 1)
    def _():
        half = CMP_STRIDE * HEAD_DIM
        for y in range(2):
            pe = jnp.broadcast_to(pe_ref[y], (8, 2 * half)).astype(BF16)
            c0 = (_dot(pe[:, :half], w1_ref[:, y * 256:y * 256 + HEAD_DIM])
                  + _dot(pe[:, half:], w1_ref[:, y * 256 + HEAD_DIM:(y + 1) * 256]))[0:1]
            for g in range(N_KV):
                yg = y * N_KV + g
                hid = (a_ref[pl.ds(yg, n16, stride=rpt), :] + b_ref[pl.ds(rpt + yg, n16, stride=rpt), :] + c0)
                o = _dot(jax.nn.gelu(hid).astype(BF16), w2_ref[y])
                if y == 0:
                    kc_ref[0, g] = o.astype(kc_ref.dtype)
                else:
                    vc_ref[0, g] = o.astype(vc_ref.dtype)


def _compress_pages(cache, page_specs, pt, nb, nsteps, w1cat, pe, w2, layer):
    n_in = len(page_specs)
    n16 = nsteps * n_in * (PAGE // CMP_STRIDE)
    rpt = 2 * N_KV
    grid_spec = pltpu.PrefetchScalarGridSpec(
        num_scalar_prefetch=1,
        grid=(nb, nsteps),
        in_specs=list(page_specs) + [
            pl.BlockSpec((None, 2048, 512), lambda b, s, p: (layer, 0, 0)),
            pl.BlockSpec((None, 2, 1, 4096), lambda b, s, p: (layer, 0, 0, 0)),
            pl.BlockSpec((None, 2, HEAD_DIM, HEAD_DIM), lambda b, s, p: (layer, 0, 0, 0)),
        ],
        out_specs=[pl.BlockSpec((1, N_KV, n16, HEAD_DIM), lambda b, s, p: (b, 0, 0, 0)),
                   pl.BlockSpec((1, N_KV, n16, HEAD_DIM), lambda b, s, p: (b, 0, 0, 0))],
        scratch_shapes=[pltpu.VMEM(((n16 + 8) * rpt, HEAD_DIM), F32), pltpu.VMEM(((n16 + 8) * rpt, HEAD_DIM), F32)],
    )
    return pl.pallas_call(
        functools.partial(_compress_pages_kernel, n_in=n_in, nsteps=nsteps, n16=n16),
        name="compress_pages",
        grid_spec=grid_spec,
        out_shape=[jax.ShapeDtypeStruct((nb, N_KV, n16, HEAD_DIM), BF16),
                   jax.ShapeDtypeStruct((nb, N_KV, n16, HEAD_DIM), BF16)],
        compiler_params=_cparams("parallel", "arbitrary"),
    )(pt, *([cache] * n_in), w1cat, pe, w2)


def _bucket_np(d):
    n = np.maximum(d, 0)
    exact = N_BUCKETS // 2
    nf = np.maximum(n, 1).astype(np.float64)
    frac = np.log(nf / exact) / math.log(MAX_DISTANCE / exact) * (N_BUCKETS - exact)
    near_int = np.abs(frac - np.round(frac)) < 1e-6
    assert not np.any(near_int & (n > exact) & (n < MAX_DISTANCE))
    large = exact + np.floor(frac + 1e-9).astype(np.int64)
    return np.where(n < exact, n, np.minimum(large, N_BUCKETS - 1)).astype(np.int32)


def _idx_np(d, ok):
    return np.where(ok, _bucket_np(d), -1).astype(np.int32)


def _table_kernel(rb_ref, idx_ref, o_ref):
    idx = idx_ref[...]
    for h in range(N_HEADS):
        far = rb_ref[N_BUCKETS - 1, h]
        acc = jnp.zeros(idx.shape, F32)
        for bkt in range(N_BUCKETS - 1):
            acc = jnp.where(idx == bkt, (rb_ref[bkt, h] - far) * LOG2E, acc)
        o_ref[h] = jnp.where(idx < 0, NEG, acc)


def _bias_table(rel_bias, idx):
    r, c = idx.shape
    return pl.pallas_call(
        _table_kernel,
        name="bias_table",
        in_specs=[pl.BlockSpec(memory_space=pltpu.SMEM), pl.BlockSpec((r, c), lambda: (0, 0))],
        out_specs=pl.BlockSpec((N_HEADS, r, c), lambda: (0, 0, 0)),
        out_shape=jax.ShapeDtypeStruct((N_HEADS, r, c), F32),
    )(rel_bias, jnp.asarray(idx))


def _prompt_tables(rel_bias):
    t = np.arange(TQ)[:, None]
    k = np.arange(256)[None, :]
    d = 64 * (3 - k // 64) + t - (k % 64)
    btn = _bias_table(rel_bias, _idx_np(d, d >= 0))
    d = 16 * (12 - k) + t - 31
    cb = _bias_table(rel_bias, np.where(k < 16, _idx_np(d, d >= 0), N_BUCKETS - 1).astype(np.int32))
    g = N_KV
    return btn.reshape(g, Q_PER_KV * TQ, 256), cb.reshape(g, Q_PER_KV * TQ, 256)


def _window_mid_mask_np():
    t = np.arange(TQ)[:, None]
    k = np.arange(512)[None, :]
    d = 64 * (11 - k // 64) + t - (k % 64)
    m = np.where(d < WINDOW, 0.0, NEG).astype(np.float32)
    return np.tile(m, (Q_PER_KV, 1))


def _sel_matrix_np(n_slc_pad, nc_pad, nc):
    j = np.arange(n_slc_pad)[:, None]
    c = np.arange(nc_pad)[None, :]
    return ((c >= 4 * j - 1) & (c <= 4 * j + 3) & (c < nc)).astype(np.float32)


def _online(s, v, m, l, acc):
    m_new = jnp.maximum(m, jnp.max(s, axis=1, keepdims=True))
    a = jnp.exp2(m - m_new)
    p = jnp.exp2(s - m_new)
    l = a * l + jnp.sum(p, axis=1, keepdims=True)
    acc = a * acc + _dot(p.astype(BF16), v)
    return m_new, l, acc


def _first(s, v):
    m = jnp.max(s, axis=1, keepdims=True)
    p = jnp.exp2(s - m)
    return m, jnp.sum(p, axis=1, keepdims=True), _dot(p.astype(BF16), v)


def _online_aug(s, v_aug, m, acc):
    m_new = jnp.maximum(m, jnp.max(s, axis=1, keepdims=True))
    acc = jnp.exp2(m - m_new) * acc + _dot(jnp.exp2(s - m_new).astype(BF16), v_aug)
    return m_new, acc


def _first_aug(s, v_aug):
    m = jnp.max(s, axis=1, keepdims=True)
    return m, _dot(jnp.exp2(s - m).astype(BF16), v_aug)


def _normalise_aug(acc):
    return acc[:, :HEAD_DIM] * (1.0 / acc[:, HEAD_DIM:])


def _rank_select(sc, rivals, jj, cur):
    cnt = jnp.zeros(sc.shape, F32)
    for j2 in range(rivals.shape[0]):
        row = rivals[j2:j2 + 1, :]
        beats = (row > sc) | ((row == sc) & (jj > j2))
        cnt = cnt + jnp.where(beats, 1.0, 0.0)
    return jnp.where((cnt < N_SEL - 0.5) & (jj <= cur), 1.0, 0.0)


def _attn_prompt_kernel(*refs):
    def tile(i, carry):
        _attn_prompt_tile(i, *refs)
        return carry

    lax.fori_loop(0, refs[0].shape[1] // TQ, tile, 0, unroll=2)


def _attn_prompt_tile(i, q_ref, gt_ref, kc_ref, vc_ref, ks_ref, vs_ref, kw_ref, vw_ref,
                      btn_ref, cb_ref, pm_ref, st_ref, et_ref, pad_ref, one_ref, o_ref):
    rows = Q_PER_KV * TQ
    t0 = pl.multiple_of(i * TQ, TQ)
    q = q_ref[0, pl.ds(t0, TQ), :]
    qg = jnp.concatenate([q[:, r * HEAD_DIM:(r + 1) * HEAD_DIM] for r in range(Q_PER_KV)], axis=0)

    lane = lax.broadcasted_iota(jnp.int32, (rows, 256), 1)
    cbr = pltpu.roll(cb_ref[0], lax.rem(4 * i + 244, 256), 1)
    s = _dot_nt(qg, kc_ref[0, 0]) * C1 + jnp.where(lane >= 4 * i + 4, NEG, cbr)
    mx = jnp.max(s, axis=1, keepdims=True)
    mx = jnp.where(mx > 0.5 * NEG, mx, 0.0)
    e = jnp.exp2(s - mx)
    p = e * (1.0 / jnp.maximum(jnp.sum(e, axis=1, keepdims=True), 1e-30))
    o_c = _dot(p.astype(BF16), vc_ref[0, 0])

    imp = p[0:TQ] + p[TQ:2 * TQ] + p[2 * TQ:3 * TQ] + p[3 * TQ:4 * TQ]
    st = st_ref[...]
    h1, h2, h3 = _split3(imp)
    ps_t = _dot_nt(st, h1) + _dot_nt(st, h2) + _dot_nt(st, h3)
    jj = lax.broadcasted_iota(jnp.int32, (64, TQ), 0)
    forced = (jj == 0) | (jj == i) | (jj == i - 1)
    sc = jnp.where(jj > i, NEG_BIG, jnp.where(forced, POS_BIG, ps_t))
    sel_t = _rank_select(sc, sc, jj, i)
    eye = jnp.where(lax.broadcasted_iota(jnp.int32, (TQ, TQ), 0) == lax.broadcasted_iota(jnp.int32, (TQ, TQ), 1),
                    1.0, 0.0).astype(BF16)
    sel_t = jnp.concatenate([sel_t, jnp.zeros((64, TQ), F32)], axis=0).astype(BF16)
    sel = _dot_nt(eye, sel_t)

    jl = lax.broadcasted_iota(jnp.int32, (TQ, 128), 1)

    def mask_lanes(keep):
        m = jnp.where(jl <= 64, 1.0 - keep, 0.0) * NEG
        return jnp.concatenate([m] * Q_PER_KV, axis=0).astype(BF16)

    q_near = jnp.concatenate([qg, mask_lanes(sel)], axis=1)
    q_far = jnp.concatenate([qg, mask_lanes(jnp.where(jl <= i - 4, sel, 0.0))], axis=1)

    def keys(ref, base, n, id_ref):
        return jnp.concatenate([ref[0, pl.ds(base, n), :], id_ref[pl.ds(base, n), :]], axis=1)

    def vals(ref, base, n):
        return jnp.concatenate([ref[0, pl.ds(base, n), :], one_ref[0:n, :]], axis=1)

    btn = btn_ref[0]
    base_n = pl.multiple_of(KV_PAD + (i - 3) * 64, 64)
    base_m = pl.multiple_of(KV_PAD + (i - 11) * 64, 64)

    s = _dot_nt(q_near, jnp.concatenate([keys(ks_ref, base_n, 256, et_ref), keys(kw_ref, base_m, 768, pad_ref)],
                                        axis=0)) * C1
    m_s, a_s = _first_aug(s[:, 0:256] + btn, vals(vs_ref, base_n, 256))
    _, a_w = _first_aug(s[:, 256:1024] + jnp.concatenate([pm_ref[...], btn], axis=1), vals(vw_ref, base_m, 768))
    o_w = _normalise_aug(a_w)


    def far_scores(c):
        base = pl.multiple_of(KV_PAD + c * 512, 512)
        return _dot_nt(q_far, keys(ks_ref, base, 512, et_ref)) * C1, vals(vs_ref, base, 512)

    def far_group(k, first, carry):
        m, acc = carry
        sv = [far_scores(first + u) for u in range(k)]
        m_new = m
        for s_u, _ in sv:
            m_new = jnp.maximum(m_new, jnp.max(s_u, axis=1, keepdims=True))
        acc = jnp.exp2(m - m_new) * acc
        for s_u, v_u in sv:
            acc = acc + _dot(jnp.exp2(s_u - m_new).astype(BF16), v_u)
        return m_new, acc

    n_far = jnp.where(i >= 4, (i - 4) // 8 + 1, 0)
    n3 = n_far // 3
    rest = n_far - 3 * n3
    m_s, a_s = lax.fori_loop(0, n3, lambda c3, carry: far_group(3, 3 * c3, carry), (m_s, a_s))
    m_s, a_s = lax.cond(rest == 2, lambda carry: far_group(2, 3 * n3, carry), lambda carry: carry, (m_s, a_s))
    m_s, a_s = lax.cond(rest == 1, lambda carry: far_group(1, 3 * n3, carry), lambda carry: carry, (m_s, a_s))
    o_s = _normalise_aug(a_s)

    gt = jax.nn.sigmoid(gt_ref[0, 0, pl.ds(t0, TQ), :])

    def gcol(br):
        return jnp.concatenate([gt[:, 3 * r + br:3 * r + br + 1] for r in range(Q_PER_KV)], axis=0)

    out = gcol(0) * o_c + gcol(1) * o_s + gcol(2) * o_w
    for r in range(Q_PER_KV):
        o_ref[0, pl.ds(t0, TQ), r * HEAD_DIM:(r + 1) * HEAD_DIM] = out[r * TQ:(r + 1) * TQ].astype(o_ref.dtype)


def _attn_prompt(q, gt, kc, vc, ks, kw, btn, cb, pm, st):
    b, t, _ = q.shape
    lp = ks.shape[1]
    nc = kc.shape[2]
    assert nc == 256 and t // SEL_BLOCK == 64
    kspec = pl.BlockSpec((1, lp, HEAD_DIM), lambda bb, g: (bb, 0, g))
    vspec = pl.BlockSpec((1, lp, HEAD_DIM), lambda bb, g: (bb, 0, N_KV + g))
    row = np.arange(lp)[:, None]
    lane = np.arange(128)[None, :]
    et = jnp.asarray(np.where(row >= KV_PAD, lane == (row - KV_PAD) // SEL_BLOCK, lane == 64).astype(np.float32), BF16)
    pad = jnp.asarray(((row < KV_PAD) & (lane == 64)).astype(np.float32), BF16)
    ones = jnp.ones((768, 128), BF16)
    return pl.pallas_call(
        _attn_prompt_kernel,
        name="attn_prompt",
        grid=(b, N_KV),
        in_specs=[
            pl.BlockSpec((1, t, Q_PER_KV * HEAD_DIM), lambda bb, g: (bb, 0, g)),
            pl.BlockSpec((1, 1, t, 12), lambda bb, g: (bb, g, 0, 0)),
            pl.BlockSpec((1, 1, nc, HEAD_DIM), lambda bb, g: (bb, g, 0, 0)),
            pl.BlockSpec((1, 1, nc, HEAD_DIM), lambda bb, g: (bb, g, 0, 0)),
            kspec, vspec, kspec, vspec,
            pl.BlockSpec((1, Q_PER_KV * TQ, 256), lambda bb, g: (g, 0, 0)),
            pl.BlockSpec((1, Q_PER_KV * TQ, 256), lambda bb, g: (g, 0, 0)),
            pl.BlockSpec((Q_PER_KV * TQ, 512), lambda bb, g: (0, 0)),
            pl.BlockSpec((64, 256), lambda bb, g: (0, 0)),
            pl.BlockSpec((lp, 128), lambda bb, g: (0, 0)),
            pl.BlockSpec((lp, 128), lambda bb, g: (0, 0)),
            pl.BlockSpec((768, 128), lambda bb, g: (0, 0)),
        ],
        out_specs=pl.BlockSpec((1, t, Q_PER_KV * HEAD_DIM), lambda bb, g: (bb, 0, g)),
        out_shape=jax.ShapeDtypeStruct((b, t, N_HEADS * HEAD_DIM), BF16),
        compiler_params=_cparams("parallel", "parallel"),
    )(q, gt, kc, vc, ks, ks, kw, kw, btn, cb, pm, st, et, pad, ones)


S_DEC = 8
S_ROWS = N_HEADS * S_DEC
PAGES_PER_STEP = 8
N_PAGES = PAST_LEN // PAGE
N_STEPS = N_PAGES // PAGES_PER_STEP
NC_S = PAST_LEN // CMP_STRIDE
CUR_S = PAST_LEN // SEL_BLOCK
NJ_S = 264
GW = N_KV * HEAD_DIM


def _diag_blocks(x):
    return jnp.concatenate([x[g * 32:(g + 1) * 32, g * HEAD_DIM:(g + 1) * HEAD_DIM] for g in range(N_KV)], axis=0)


def _attn_sample_kernel(pt_ref, q_ref, gt_ref, kc_ref, vc_ref, *refs):
    del pt_ref
    pages = refs[:PAGES_PER_STEP]
    (snew_ref, wbuf_ref, wnew_ref, tc_ref, tl_ref, tn_ref, tw_ref, st_ref, e16_ref,
     o_ref, qb_ref, sc_ref, sel_ref, m_ref, l_ref, acc_ref, oc_ref) = refs[PAGES_PER_STEP:]
    step = pl.program_id(1)

    @pl.when(step == 0)
    def _():
        q = q_ref[0].astype(F32)
        zero = jnp.zeros((32, HEAD_DIM), F32)
        qgs, qb_rows = [], []
        for g in range(N_KV):
            qg = jnp.concatenate([q[:, (4 * g + r) * HEAD_DIM:(4 * g + r + 1) * HEAD_DIM]
                                  for r in range(Q_PER_KV)], axis=0)
            qgs.append(qg.astype(BF16))
            qb_rows.append(jnp.concatenate([qg if gg == g else zero for gg in range(N_KV)], axis=1))
        qb = jnp.concatenate(qb_rows, axis=0).astype(BF16)
        qb_ref[...] = qb

        s = jnp.concatenate([_dot_nt(qgs[g], kc_ref[0, g]) for g in range(N_KV)], axis=0) * C1 + tc_ref[...]
        mx = jnp.max(s, axis=1, keepdims=True)
        mx = jnp.where(mx > 0.5 * NEG, mx, 0.0)
        e = jnp.exp2(s - mx)
        p = e * (1.0 / jnp.maximum(jnp.sum(e, axis=1, keepdims=True), 1e-30))
        pb = p.astype(BF16)
        oc_ref[...] = jnp.concatenate([_dot(pb[g * 32:(g + 1) * 32], vc_ref[0, g]) for g in range(N_KV)], axis=0)

        reps = []
        for g in range(N_KV):
            pg = p[g * 32:(g + 1) * 32]
            ig = pg[0:8] + pg[8:16] + pg[16:24] + pg[24:32]
            reps += [ig] * Q_PER_KV
        imp = jnp.concatenate(reps, axis=0)
        st = st_ref[...]
        h1, h2, h3 = _split3(imp)
        ps_t = _dot_nt(st, h1) + _dot_nt(st, h2) + _dot_nt(st, h3)
        jj = lax.broadcasted_iota(jnp.int32, (NJ_S, S_ROWS), 0)
        forced = (jj == 0) | (jj == CUR_S) | (jj == CUR_S - 1)
        sc = jnp.where(jj > CUR_S, NEG_BIG, jnp.where(forced, POS_BIG, ps_t))
        sc_ref[...] = sc
        sel_ref[...] = jnp.zeros((NJ_S, S_ROWS), F32)
        jf = jj.astype(F32)

        def take_best(_, carry):
            v = sc_ref[...]
            best = jnp.max(v, axis=0, keepdims=True)
            first = jnp.min(jnp.where(v == best, jf, float(NJ_S)), axis=0, keepdims=True)
            hit = jf == first
            sel_ref[...] = jnp.where(hit, 1.0, sel_ref[...])
            sc_ref[...] = jnp.where(hit, 2.0 * NEG_BIG, v)
            return carry

        lax.fori_loop(0, N_SEL, take_best, 0)
        sel_ref[...] = jnp.where(jj <= CUR_S, sel_ref[...], 0.0)

        new = jnp.concatenate([snew_ref[0], jnp.zeros((128 - S_DEC, 2 * GW), F32)], axis=0).astype(BF16)
        sn = _dot_nt(qb, new[:, :GW]) * C1 + tn_ref[...]
        m0, l0, a0 = _first(sn, new[:, GW:])
        m_ref[...] = m0
        l_ref[...] = l0
        acc_ref[...] = a0

    def gather_rows(refs, n_tok, first):
        return jnp.concatenate(
            [jnp.concatenate([r[0, pl.ds(first + g, n_tok, stride=2 * N_KV), :] for r in refs], axis=0)
             for g in range(N_KV)], axis=1)

    kb = gather_rows(pages, PAGE, 0).astype(BF16)
    vb = gather_rows(pages, PAGE, N_KV).astype(BF16)
    qb = qb_ref[...]
    s = _dot_nt(qb, kb) * C1 + jnp.where(step == N_STEPS - 1, 1.0, 0.0) * tl_ref[...]
    blocks = PAGES_PER_STEP * PAGE // SEL_BLOCK
    sel_step = sel_ref[pl.ds(pl.multiple_of(step * blocks, blocks), blocks), :]
    eye = jnp.where(lax.broadcasted_iota(jnp.int32, (S_ROWS, S_ROWS), 0)
                    == lax.broadcasted_iota(jnp.int32, (S_ROWS, S_ROWS), 1), 1.0, 0.0).astype(BF16)
    sel_r = _dot_nt(eye, sel_step.astype(BF16)).astype(BF16)
    s = jnp.where(_dot(sel_r, e16_ref[...]) > 0.5, s, NEG)
    m1, l1, a1 = _online(s, vb, m_ref[...], l_ref[...], acc_ref[...])
    m_ref[...] = m1
    l_ref[...] = l1
    acc_ref[...] = a1

    @pl.when(step == N_STEPS - 1)
    def _():
        o_s = _diag_blocks(a1 * (1.0 / l1))
        wn = jnp.concatenate([wnew_ref[0], jnp.zeros((128 - S_DEC, 2 * GW), F32)], axis=0)
        wk = jnp.concatenate([gather_rows([wbuf_ref], WINDOW, 0), wn[:, :GW]], axis=0).astype(BF16)
        wv = jnp.concatenate([gather_rows([wbuf_ref], WINDOW, N_KV), wn[:, GW:]], axis=0).astype(BF16)
        sw = _dot_nt(qb, wk) * C1 + tw_ref[...]
        _, lw, aw = _first(sw, wv)
        o_w = _diag_blocks(aw * (1.0 / lw))
        gt = jax.nn.sigmoid(gt_ref[0])
        out = gt[:, 0:1] * oc_ref[...] + gt[:, 1:2] * o_s + gt[:, 2:3] * o_w
        for h in range(N_HEADS):
            o_ref[0, :, h * HEAD_DIM:(h + 1) * HEAD_DIM] = out[h * S_DEC:(h + 1) * S_DEC]


def _attn_sample(q, gt, kc, vc, cache, page_table, page_base, snew, wbuf, wnew, layer, tabs):
    b = q.shape[0]
    tc, tl, tn, tw, st, e16 = tabs
    rows_per_tok = 2 * N_KV

    def page_spec(j):
        return pl.BlockSpec((1, PAGE * rows_per_tok, HEAD_DIM),
                            lambda bb, s, pt: (page_base + pt[bb, s * PAGES_PER_STEP + j], 0, 0))

    def full(shape):
        return pl.BlockSpec(shape, lambda bb, s, pt: tuple(0 for _ in shape))

    grid_spec = pltpu.PrefetchScalarGridSpec(
        num_scalar_prefetch=1,
        grid=(b, N_STEPS),
        in_specs=[
            pl.BlockSpec((1, S_DEC, N_HEADS * HEAD_DIM), lambda bb, s, pt: (bb, 0, 0)),
            pl.BlockSpec((1, S_ROWS, 3), lambda bb, s, pt: (bb, 0, 0)),
            pl.BlockSpec((1, N_KV, NC_S, HEAD_DIM), lambda bb, s, pt: (bb, 0, 0, 0)),
            pl.BlockSpec((1, N_KV, NC_S, HEAD_DIM), lambda bb, s, pt: (bb, 0, 0, 0)),
        ] + [page_spec(j) for j in range(PAGES_PER_STEP)] + [
            pl.BlockSpec((1, S_DEC, 2 * GW), lambda bb, s, pt: (bb, 0, 0)),
            pl.BlockSpec((None, 1, WINDOW * rows_per_tok, HEAD_DIM), lambda bb, s, pt: (layer, bb, 0, 0)),
            pl.BlockSpec((1, S_DEC, 2 * GW), lambda bb, s, pt: (bb, 0, 1)),
            full(tc.shape), full(tl.shape), full(tn.shape), full(tw.shape), full(st.shape), full(e16.shape),
        ],
        out_specs=pl.BlockSpec((1, S_DEC, N_HEADS * HEAD_DIM), lambda bb, s, pt: (bb, 0, 0)),
        scratch_shapes=[pltpu.VMEM((S_ROWS, GW), BF16), pltpu.VMEM((NJ_S, S_ROWS), F32),
                        pltpu.VMEM((NJ_S, S_ROWS), F32), pltpu.VMEM((S_ROWS, 1), F32), pltpu.VMEM((S_ROWS, 1), F32),
                        pltpu.VMEM((S_ROWS, GW), F32), pltpu.VMEM((S_ROWS, HEAD_DIM), F32)],
    )
    return pl.pallas_call(
        _attn_sample_kernel,
        name="attn_sample",
        grid_spec=grid_spec,
        out_shape=jax.ShapeDtypeStruct((b, S_DEC, N_HEADS * HEAD_DIM), F32),
        compiler_params=_cparams("parallel", "arbitrary"),
    )(page_table, q, gt, kc, vc, *([cache] * PAGES_PER_STEP), snew, wbuf, wnew, tc, tl, tn, tw, st, e16)


def _sample_tables(rel_bias):
    s = np.arange(S_DEC)[:, None]
    qpos = PAST_LEN + s

    def rows(tab):
        return tab.reshape(S_ROWS, tab.shape[-1])

    c = np.arange(NC_S)[None, :]
    d = qpos - (CMP_STRIDE * c + 2 * CMP_STRIDE - 1)
    tc = rows(_bias_table(rel_bias, _idx_np(d, d >= 0)))
    k = np.arange(PAGES_PER_STEP * PAGE)[None, :]
    d = qpos - (PAST_LEN - PAGES_PER_STEP * PAGE + k)
    tl = rows(_bias_table(rel_bias, _idx_np(d, d >= 0)))
    k = np.arange(128)[None, :]
    d = s - k
    tn = rows(_bias_table(rel_bias, _idx_np(d, (d >= 0) & (k < S_DEC))))
    k = np.arange(WINDOW + 128)[None, :]
    d = np.where(k < WINDOW, qpos - (PAST_LEN - WINDOW + k), s - (k - WINDOW))
    tw = rows(_bias_table(rel_bias, _idx_np(d, (d >= 0) & (d < WINDOW) & (k < WINDOW + S_DEC))))
    st = jnp.asarray(_sel_matrix_np(NJ_S, NC_S, NC_S - 1), BF16)
    blocks = PAGES_PER_STEP * PAGE // SEL_BLOCK
    e16 = jnp.asarray((np.arange(blocks)[:, None] == np.arange(PAGES_PER_STEP * PAGE)[None, :] // SEL_BLOCK)
                      .astype(np.float32), BF16)
    return tc, tl, tn, tw, st, e16


def _projections(hf, w_main, w_gn, w_gm, layer):
    za = _mm(hf, w_main, layer, 0, 4096, F32, w_t=True)
    q = _mm(hf, w_main, layer, 4096, 2048, BF16, w_t=True)
    cmp = _mm(hf, w_main, layer, 6144, 1024, F32, w_t=True)
    sw = _mm(hf, w_main, layer, 7168, 2048, F32, w_t=True)
    gn = _mm(hf, w_gn, layer, 0, 48, F32, w_t=True)
    gm = _mm(hf, w_gm, layer, N_MAIN + 48, 3 * D_MODEL, F32, w_t=True)
    return za, q, cmp, sw, gn, gm


def kernel(x_prompt, x_sample, cache_cmp_kv, cache_slc_kv, state_win_kv, state_conv, state_pool, page_table,
           c_prompt, c_sample, rel_bias, norm_g, ada_w, ada_b, w_in, conv_w, pool_w, pool_scale,
           cmp_pe, cmp_w1, cmp_w2, w_branch, w_out, mlp_w1, mlp_w2):
    bp, t, d = x_prompt.shape
    bs, s_len, _ = x_sample.shape
    n_phys = cache_cmp_kv.shape[1]
    assert (t, d, s_len, bs) == (4096, D_MODEL, S_DEC, 8)

    w_main = jnp.swapaxes(w_in, 1, 2)
    w_gn = w_main[:, N_MAIN:N_MAIN + 48]
    w_gm = w_main
    ada_wb = ada_w
    w_branch_b = w_branch.astype(BF16)
    w_out_b = w_out
    w1_b = mlp_w1
    w2_b = mlp_w2
    pool_wb = pool_w.astype(BF16)
    w1r = cmp_w1.reshape(DEPTH, 2, 2 * CMP_STRIDE, HEAD_DIM, HEAD_DIM)
    half = CMP_STRIDE * HEAD_DIM
    w1ab = jnp.concatenate([w1r[:, :, :CMP_STRIDE].reshape(DEPTH, 2, half, HEAD_DIM),
                            w1r[:, :, CMP_STRIDE:].reshape(DEPTH, 2, half, HEAD_DIM)], axis=-1).astype(BF16)
    w1cat = jnp.concatenate([w1ab[:, 0], w1ab[:, 1]], axis=-1)
    pe = cmp_pe.reshape(DEPTH, 2, 1, 2 * half)
    w2c = cmp_w2.astype(BF16)

    c16 = jnp.concatenate([c_prompt, c_sample, jnp.zeros((16 - bp - bs, d), F32)], axis=0)
    mod = _ada(c16, ada_wb, ada_b)
    mod_p = mod[:, :bp].reshape(DEPTH, bp, 6, d)
    mod_s = mod[:, bp:bp + bs].reshape(DEPTH, bs, 6, d)

    btn, cb = _prompt_tables(rel_bias)
    pm = jnp.asarray(_window_mid_mask_np())
    st_p = jnp.asarray(_sel_matrix_np(t // SEL_BLOCK, t // CMP_STRIDE, t // CMP_STRIDE - 1), BF16)
    tabs_s = _sample_tables(rel_bias)

    rpt = 2 * N_KV
    cache_c = cache_cmp_kv.reshape(DEPTH * n_phys, PAGE * rpt, HEAD_DIM)
    cache_s = cache_slc_kv.reshape(DEPTH * n_phys, PAGE * rpt, HEAD_DIM)
    wbuf = state_win_kv.reshape(DEPTH, bs, WINDOW * rpt, HEAD_DIM)
    conv0 = jnp.zeros((bp, 2, D_CONV), F32)
    pool0 = jnp.zeros((bp, 15, D_POOL), F32)
    dummy_pt = jnp.zeros((1, 1), jnp.int32)

    xp, xs = x_prompt, x_sample
    _, hp = _norm(xp, None, norm_g, mod_p, None, (0, 0, 0, 1))
    _, hs = _norm(xs, None, norm_g, mod_s, None, (0, 0, 0, 1))
    outs = [[] for _ in range(10)]
    mp, ms = bp * t, bs * s_len
    for l in range(DEPTH):
        za, q, cmp, sw, gn, gm = _projections(hp.reshape(mp, d), w_main, w_gn, w_gm, l)
        ya, yb, cst_p, pst_p = _convpool(za.reshape(bp, t, 4096), conv0, pool0, conv_w, pool_wb, pool_scale, l, 0)
        n16 = t // CMP_STRIDE
        kc, vc = _compress([cmp.reshape(bp, n16, CMP_STRIDE * 2 * GW)],
                           [pl.BlockSpec((1, 64, CMP_STRIDE * 2 * GW), lambda b, s, p: (b, s, 0))],
                           dummy_pt, bp, n16 // 64, 64, w1ab, pe, w2c, l)
        swp = sw.reshape(bp, t, 4 * GW)
        ks = jnp.pad(swp[:, :, :2 * GW].astype(BF16), ((0, 0), (KV_PAD, 0), (0, 0)))
        kw = jnp.pad(swp[:, :, 2 * GW:].astype(BF16), ((0, 0), (KV_PAD, 0), (0, 0)))
        gt = gn.reshape(bp, t, N_KV, 12).transpose(0, 2, 1, 3)
        yc = _attn_prompt(q.reshape(bp, t, 2048), gt, kc, vc, ks, kw, btn, cb, pm, st_p)
        merged = _merge(ya.reshape(mp, D_CONV), yb.reshape(mp, D_POOL), yc.reshape(mp, 2048), gm, w_branch_b, l)
        mo = _mm(merged, w_out_b, l, 0, d, F32)
        xp, h2 = _norm(xp, mo.reshape(bp, t, d), norm_g, mod_p, (l, 1, 2), (l, 2, 3, 4))
        u = _mm(h2.reshape(mp, d), w1_b, l, 0, D_FF, BF16, act="relu2")
        f = _mm(u, w2_b, l, 0, d, F32)
        xp, hp = _norm(xp, f.reshape(bp, t, d), norm_g, mod_p, (l, 3, 5), (l + 1, 0, 0, 1) if l + 1 < DEPTH else None)
        sw5 = swp.reshape(bp, t, 2, 2, N_KV, HEAD_DIM)
        outs[0].append(cmp.reshape(bp, t, 2, N_KV, HEAD_DIM))
        outs[2].append(sw5[:, :, 0])
        outs[4].append(sw5[:, t - WINDOW:, 1])
        outs[6].append(cst_p)
        outs[8].append(pst_p)

        za, q, cmp, sw, gn, gm = _projections(hs.reshape(ms, d), w_main, w_gn, w_gm, l)
        ya, yb, cst_s, pst_s = _convpool(za.reshape(bs, s_len, 4096), state_conv[l], state_pool[l],
                                         conv_w, pool_wb, pool_scale, l, PAST_LEN)
        sws = sw.reshape(bs, s_len, 4 * GW)
        page_specs = [pl.BlockSpec((1, PAGE * rpt, HEAD_DIM),
                                   lambda b, s, p, j=j, l=l: (l * n_phys + p[b, s * PAGES_PER_STEP + j], 0, 0))
                      for j in range(PAGES_PER_STEP)]
        kc, vc = _compress_pages(cache_c, page_specs, page_table, bs, N_STEPS, w1cat, pe, w2c, l)
        gts = gn.reshape(bs, s_len, N_KV, Q_PER_KV, 3).transpose(0, 2, 3, 1, 4).reshape(bs, S_ROWS, 3)
        yc = _attn_sample(q.reshape(bs, s_len, 2048), gts, kc, vc, cache_s, page_table, l * n_phys,
                          sws, wbuf, sws, l, tabs_s)
        merged = _merge(ya.reshape(ms, D_CONV), yb.reshape(ms, D_POOL), yc.reshape(ms, 2048), gm, w_branch_b, l)
        mo = _mm(merged, w_out_b, l, 0, d, F32)
        xs, h2 = _norm(xs, mo.reshape(bs, s_len, d), norm_g, mod_s, (l, 1, 2), (l, 2, 3, 4))
        u = _mm(h2.reshape(ms, d), w1_b, l, 0, D_FF, BF16, act="relu2")
        f = _mm(u, w2_b, l, 0, d, F32)
        xs, hs = _norm(xs, f.reshape(bs, s_len, d), norm_g, mod_s, (l, 3, 5), (l + 1, 0, 0, 1) if l + 1 < DEPTH else None)
        sw5 = sws.reshape(bs, s_len, 2, 2, N_KV, HEAD_DIM)
        outs[1].append(cmp.reshape(bs, s_len, 2, N_KV, HEAD_DIM))
        outs[3].append(sw5[:, :, 0])
        outs[5].append(jnp.concatenate([state_win_kv[l][:, s_len:], sw5[:, :, 1]], axis=1))
        outs[7].append(cst_s)
        outs[9].append(pst_s)

    return (xp, xs) + tuple(jnp.stack(o) for o in outs)
```
